```python
import math
import jax, jax.numpy as jnp
from jax import lax
import numpy as np

D_MODEL = 1024
BATCH = 1
SEQ = 16384
DEPTH = 1
DEC_BATCH = 128
DEC_SEQ = 8
PAST_LEN = 8192
PAGE_SIZE = 128

HEAD_DIM = 64
H_MOBA = 8
H_SB = 8
W_MOBA = H_MOBA * HEAD_DIM
W_SB = H_SB * HEAD_DIM
MOBA_BLOCK = 256
MOBA_TOPK = 3
QBLK = 128
ROPE_THETA = 10000.0
PEER_HEADS = 8
PEER_NKEYS = 128
PEER_EXPERTS = PEER_NKEYS * PEER_NKEYS
PEER_DKEY = 256
PEER_TOPK = 16
PLE_DIM = 256
TOK_BLK = 128
RMS_EPS = 1e-6
NEG_INF = -1e30
IN_COLS = 3 * W_MOBA + 3 * W_SB + 2 * D_MODEL

kernel_name = 'moba_stickbreak_peer_hybrid_step'


def rmsnorm(x, g):
    xf = x.astype(jnp.float32)
    y = xf * lax.rsqrt(jnp.mean(xf * xf, axis=-1, keepdims=True) + RMS_EPS)
    return (y * g.astype(jnp.float32)).astype(x.dtype)


def rope(x, pos):
    half = HEAD_DIM // 2
    inv = ROPE_THETA ** (-jnp.arange(half, dtype=jnp.float32) / half)
    ang = pos.astype(jnp.float32)[:, None] * inv[None, :]
    cos = jnp.cos(ang)[:, None, :]
    sin = jnp.sin(ang)[:, None, :]
    xf = x.astype(jnp.float32)
    x1, x2 = xf[..., :half], xf[..., half:]
    return jnp.concatenate([x1 * cos - x2 * sin, x2 * cos + x1 * sin], axis=-1).astype(x.dtype)


def mixer_inputs(xn, pos, w_in, q_gain, k_gain):
    N, S, _ = xn.shape
    cuts = [W_MOBA, 2 * W_MOBA, 3 * W_MOBA, 3 * W_MOBA + W_SB, 3 * W_MOBA + 2 * W_SB,
            3 * W_MOBA + 3 * W_SB, 3 * W_MOBA + 3 * W_SB + D_MODEL]
    z = jnp.einsum('nsd,dc->nsc', xn, w_in)
    qa, ka, va, qb, kb, vb, ga, gb = jnp.split(z, cuts, axis=-1)
    heads_a = lambda t: t.reshape(N, S, H_MOBA, HEAD_DIM)
    heads_b = lambda t: t.reshape(N, S, H_SB, HEAD_DIM)
    qa = rope(rmsnorm(heads_a(qa), q_gain), pos)
    ka = rope(rmsnorm(heads_a(ka), k_gain), pos)
    return qa, ka, heads_a(va), heads_b(qb), heads_b(kb), heads_b(vb), ga, gb


def moba_blocks(k, v):
    L = k.shape[0]
    nb = -(-L // MOBA_BLOCK)
    pad = ((0, nb * MOBA_BLOCK - L), (0, 0), (0, 0))
    kb = jnp.pad(k, pad).reshape(nb, MOBA_BLOCK, H_MOBA, HEAD_DIM)
    vb = jnp.pad(v, pad).reshape(nb, MOBA_BLOCK, H_MOBA, HEAD_DIM)
    kmean = jnp.mean(kb.astype(jnp.float32), axis=1)
    return kb.transpose(2, 0, 1, 3), vb.transpose(2, 0, 1, 3), kmean


def moba_attend(q, pos, kblk, vblk, kmean):
    Q = q.shape[0]
    nb = kblk.shape[1]
    kk = min(MOBA_TOPK, nb)
    qf = q.astype(jnp.float32)
    own = pos // MOBA_BLOCK
    gate = jnp.einsum('qhd,nhd->hqn', qf, kmean)
    full_past = jnp.arange(nb)[None, None, :] < own[None, :, None]
    _, sel = lax.top_k(jnp.where(full_past, gate, NEG_INF), kk)
    sel_ok = sel < own[None, :, None]
    own_b = jnp.broadcast_to(own[None, :, None], (H_MOBA, Q, 1))
    blk = jnp.concatenate([sel, own_b], axis=-1)
    ok = jnp.concatenate([sel_ok, jnp.ones((H_MOBA, Q, 1), bool)], axis=-1)
    hix = jnp.arange(H_MOBA)[:, None, None]
    kg = kblk[hix, blk].astype(jnp.float32)
    vg = vblk[hix, blk].astype(jnp.float32)
    s = jnp.einsum('qhd,hqjbd->hqjb', qf, kg) / math.sqrt(HEAD_DIM)
    kpos = blk[..., None] * MOBA_BLOCK + jnp.arange(MOBA_BLOCK)
    mask = ok[..., None] & (kpos <= pos[None, :, None, None])
    s = jnp.where(mask, s, NEG_INF)
    w = jax.nn.softmax(s.reshape(H_MOBA, Q, -1), axis=-1).reshape(s.shape)
    return jnp.einsum('hqjb,hqjbd->qhd', w, vg).astype(q.dtype)


def stickbreak_attend(q, pos, k, v):
    L = k.shape[0]
    z = jnp.einsum('qhd,khd->hqk', q.astype(jnp.float32), k.astype(jnp.float32)) / math.sqrt(HEAD_DIM)
    mask = jnp.arange(L)[None, None, :] < pos[None, :, None]
    log_beta = jax.nn.log_sigmoid(z)
    log_keep = jnp.where(mask, log_beta - z, 0.0)
    after = lax.cumsum(log_keep, axis=2, reverse=True) - log_keep
    a = jnp.where(mask, jnp.exp(log_beta + after), 0.0)
    return jnp.einsum('hqk,khd->qhd', a, v.astype(jnp.float32)).astype(q.dtype)


def prompt_mixers(qa, ka, va, qb, kb, vb):
    B, S = qa.shape[:2]
    nq = S // QBLK
    kblk, vblk, kmean = jax.vmap(moba_blocks)(ka, va)
    pos = jnp.arange(S, dtype=jnp.int32).reshape(nq, QBLK)
    to_blocks = lambda t: jnp.swapaxes(t.reshape(B, nq, QBLK, *t.shape[2:]), 0, 1)

    def step(args):
        qa_i, qb_i, pos_i = args
        oa = jax.vmap(moba_attend, in_axes=(0, None, 0, 0, 0))(qa_i, pos_i, kblk, vblk, kmean)
        ob = jax.vmap(stickbreak_attend, in_axes=(0, None, 0, 0))(qb_i, pos_i, kb, vb)
        return oa, ob

    oa, ob = lax.map(step, (to_blocks(qa), to_blocks(qb), pos))
    from_blocks = lambda t: jnp.swapaxes(t, 0, 1).reshape(B, S, *t.shape[3:])
    return from_blocks(oa), from_blocks(ob)


def sample_mixers(qa, ka, va, qb, kb, vb, cache_k_moba, cache_v_moba, cache_k_sb, cache_v_sb,
                  page_table, layer, pos):
    past = page_table.shape[1] * PAGE_SIZE

    def read(cache, pages):
        rows = cache[layer, pages]
        return rows.reshape(past, *rows.shape[2:])

    def step(args):
        qa_i, ka_i, va_i, qb_i, kb_i, vb_i, pages = args
        k_a = jnp.concatenate([read(cache_k_moba, pages), ka_i], axis=0)
        v_a = jnp.concatenate([read(cache_v_moba, pages), va_i], axis=0)
        k_b = jnp.concatenate([read(cache_k_sb, pages), kb_i], axis=0)
        v_b = jnp.concatenate([read(cache_v_sb, pages), vb_i], axis=0)
        oa = moba_attend(qa_i, pos, *moba_blocks(k_a, v_a))
        ob = stickbreak_attend(qb_i, pos, k_b, v_b)
        return oa, ob

    return lax.map(step, (qa, ka, va, qb, kb, vb, page_table))


def peer_ffn(xn, w_q, sub_keys, u_tab, v_tab):
    T, D = xn.shape
    nblk = -(-T // TOK_BLK)
    xp = jnp.pad(xn, ((0, nblk * TOK_BLK - T), (0, 0))).reshape(nblk, TOK_BLK, D)
    half = PEER_DKEY // 2

    def blk(xb):
        q = jnp.einsum('td,dc->tc', xb, w_q).reshape(TOK_BLK, PEER_HEADS, 2, half).astype(jnp.float32)
        s1 = jnp.einsum('thc,hnc->thn', q[:, :, 0], sub_keys[:, 0].astype(jnp.float32))
        s2 = jnp.einsum('thc,hnc->thn', q[:, :, 1], sub_keys[:, 1].astype(jnp.float32))
        v1, i1 = lax.top_k(s1, PEER_TOPK)
        v2, i2 = lax.top_k(s2, PEER_TOPK)
        cand = (v1[..., :, None] + v2[..., None, :]).reshape(TOK_BLK, PEER_HEADS, -1)
        cidx = (i1[..., :, None] * PEER_NKEYS + i2[..., None, :]).reshape(TOK_BLK, PEER_HEADS, -1)
        sc, j = lax.top_k(cand, PEER_TOPK)
        eidx = jnp.take_along_axis(cidx, j, axis=-1)
        g = jax.nn.softmax(sc, axis=-1)
        act = jax.nn.gelu(jnp.einsum('thkd,td->thk', u_tab[eidx].astype(jnp.float32),
                                     xb.astype(jnp.float32)), approximate=False)
        out = jnp.einsum('thk,thkd->td', g * act, v_tab[eidx].astype(jnp.float32))
        return out.astype(xn.dtype)

    return lax.map(blk, xp).reshape(nblk * TOK_BLK, D)[:T]


def post_mix(x, oa, ob, ga, gb, p, w_up_a, w_up_b, w_o, g_ffn, w_pq, sub_keys, u_tab, v_tab,
             g_ple, w_ple_gate, w_ple):
    N, S, D = x.shape
    ya = jnp.einsum('nsc,cd->nsd', oa.reshape(N, S, W_MOBA), w_up_a)
    yb = jnp.einsum('nsc,cd->nsd', ob.reshape(N, S, W_SB), w_up_b)
    merged = jax.nn.sigmoid(ga) * ya + jax.nn.sigmoid(gb) * yb
    h = x + jnp.einsum('nsd,de->nse', merged, w_o)
    hn = rmsnorm(h, g_ffn)
    h = h + peer_ffn(hn.reshape(N * S, D), w_pq, sub_keys, u_tab, v_tab).reshape(N, S, D)
    gate = jax.nn.sigmoid(jnp.einsum('nsd,de->nse', rmsnorm(h, g_ple), w_ple_gate))
    return h + gate * jnp.einsum('nsp,pd->nsd', p, w_ple)


def setup_inputs(seed: int = 0) -> dict:
    key = jax.random.key(seed)
    ks = jax.random.split(key, 32)
    n_pages = PAST_LEN // PAGE_SIZE
    used = DEC_BATCH * n_pages
    n_phys = used + max(1, used // 4)
    nrm = lambda k, shape, scale: jax.random.normal(k, shape, jnp.float32) * scale
    gain = lambda k, shape: 1.0 + 0.05 * jax.random.normal(k, shape, jnp.float32)
    page_table = jax.random.permutation(ks[6], n_phys)[:used].reshape(DEC_BATCH, n_pages).astype(jnp.int32)
    return {
        'x_prompt': nrm(ks[0], (BATCH, SEQ, D_MODEL), 1.0),
        'x_sample': nrm(ks[1], (DEC_BATCH, DEC_SEQ, D_MODEL), 1.0),
        'cache_k_moba': nrm(ks[2], (DEPTH, n_phys, PAGE_SIZE, H_MOBA, HEAD_DIM), 1.0),
        'cache_v_moba': nrm(ks[3], (DEPTH, n_phys, PAGE_SIZE, H_MOBA, HEAD_DIM), 1.0),
        'cache_k_sb': nrm(ks[4], (DEPTH, n_phys, PAGE_SIZE, H_SB, HEAD_DIM), 1.0),
        'cache_v_sb': nrm(ks[5], (DEPTH, n_phys, PAGE_SIZE, H_SB, HEAD_DIM), 1.0),
        'page_table': page_table,
        'p_prompt': nrm(ks[7], (DEPTH, BATCH, SEQ, PLE_DIM), 1.0),
        'p_sample': nrm(ks[8], (DEPTH, DEC_BATCH, DEC_SEQ, PLE_DIM), 1.0),
        'norm_mix': gain(ks[9], (DEPTH, D_MODEL)),
        'w_in': nrm(ks[10], (DEPTH, D_MODEL, IN_COLS), D_MODEL ** -0.5),
        'q_norm_moba': gain(ks[11], (DEPTH, HEAD_DIM)),
        'k_norm_moba': gain(ks[12], (DEPTH, HEAD_DIM)),
        'w_up_moba': nrm(ks[13], (DEPTH, W_MOBA, D_MODEL), W_MOBA ** -0.5),
        'w_up_sb': nrm(ks[14], (DEPTH, W_SB, D_MODEL), W_SB ** -0.5),
        'w_o': nrm(ks[15], (DEPTH, D_MODEL, D_MODEL), D_MODEL ** -0.5),
        'norm_ffn': gain(ks[16], (DEPTH, D_MODEL)),
        'w_peer_q': nrm(ks[17], (DEPTH, D_MODEL, PEER_HEADS * PEER_DKEY), D_MODEL ** -0.5),
        'peer_sub_keys': nrm(ks[18], (DEPTH, PEER_HEADS, 2, PEER_NKEYS, PEER_DKEY // 2), (PEER_DKEY // 2) ** -0.5),
        'peer_u': nrm(ks[19], (DEPTH, PEER_EXPERTS, D_MODEL), D_MODEL ** -0.5),
        'peer_v': nrm(ks[20], (DEPTH, PEER_EXPERTS, D_MODEL), PEER_HEADS ** -0.5),
        'norm_ple': gain(ks[21], (DEPTH, D_MODEL)),
        'w_ple_gate': nrm(ks[22], (DEPTH, D_MODEL, D_MODEL), D_MODEL ** -0.5),
        'w_ple': nrm(ks[23], (DEPTH, PLE_DIM, D_MODEL), PLE_DIM ** -0.5),
    }


def reference(x_prompt, x_sample, cache_k_moba, cache_v_moba, cache_k_sb, cache_v_sb, page_table,
              p_prompt, p_sample, norm_mix, w_in, q_norm_moba, k_norm_moba, w_up_moba, w_up_sb, w_o,
              norm_ffn, w_peer_q, peer_sub_keys, peer_u, peer_v, norm_ple, w_ple_gate, w_ple):
    S = x_prompt.shape[1]
    T = x_sample.shape[1]
    past = page_table.shape[1] * PAGE_SIZE
    pos_p = jnp.arange(S, dtype=jnp.int32)
    pos_s = past + jnp.arange(T, dtype=jnp.int32)
    hp, hs = x_prompt, x_sample
    kmp, vmp, ksp, vsp = [], [], [], []
    kms, vms, kss, vss = [], [], [], []
    for l in range(DEPTH):
        qa, ka, va, qb, kb, vb, ga, gb = mixer_inputs(rmsnorm(hp, norm_mix[l]), pos_p, w_in[l],
                                                      q_norm_moba[l], k_norm_moba[l])
        oa, ob = prompt_mixers(qa, ka, va, qb, kb, vb)
        hp = post_mix(hp, oa, ob, ga, gb, p_prompt[l], w_up_moba[l], w_up_sb[l], w_o[l], norm_ffn[l],
                      w_peer_q[l], peer_sub_keys[l], peer_u[l], peer_v[l], norm_ple[l], w_ple_gate[l], w_ple[l])
        kmp.append(ka); vmp.append(va); ksp.append(kb); vsp.append(vb)
        qa, ka, va, qb, kb, vb, ga, gb = mixer_inputs(rmsnorm(hs, norm_mix[l]), pos_s, w_in[l],
                                                      q_norm_moba[l], k_norm_moba[l])
        oa, ob = sample_mixers(qa, ka, va, qb, kb, vb, cache_k_moba, cache_v_moba, cache_k_sb, cache_v_sb,
                               page_table, l, pos_s)
        hs = post_mix(hs, oa, ob, ga, gb, p_sample[l], w_up_moba[l], w_up_sb[l], w_o[l], norm_ffn[l],
                      w_peer_q[l], peer_sub_keys[l], peer_u[l], peer_v[l], norm_ple[l], w_ple_gate[l], w_ple[l])
        kms.append(ka); vms.append(va); kss.append(kb); vss.append(vb)
    return (hp, hs, jnp.stack(kmp), jnp.stack(vmp), jnp.stack(ksp), jnp.stack(vsp),
            jnp.stack(kms), jnp.stack(vms), jnp.stack(kss), jnp.stack(vss))
```

```python
import functools
import math

import jax
import jax.numpy as jnp
from jax import lax
from jax.experimental import pallas as pl
from jax.experimental.pallas import tpu as pltpu

F32 = jnp.float32
BF16 = jnp.bfloat16

D_MODEL = 1024
HEAD_DIM = 64
N_HEADS = 8
W_MIX = N_HEADS * HEAD_DIM
PAGE_SIZE = 128
MOBA_BLOCK = 256
MOBA_TOPK = 3
ROPE_THETA = 10000.0
PEER_HEADS = 8
PEER_NKEYS = 128
PEER_TOPK = 16
RMS_EPS = 1e-6
NEG_INF = -1e30
LANES = 128
SB_UNDERFLOW = -104.0
VMEM_LIMIT = 56 * 1024 * 1024

TOK_TILE = 512
Q_TILE = 256


def _cparams(sem):
    return pltpu.CompilerParams(dimension_semantics=sem, vmem_limit_bytes=VMEM_LIMIT)


def _resident(shape):
    nd = len(shape)
    return pl.BlockSpec(shape, lambda *_: (0,) * nd, pipeline_mode=pl.Buffered(1))


def _split_dot(x, m):
    hi = x.astype(BF16)
    lo = (x - hi.astype(F32)).astype(BF16)
    return (jnp.dot(hi, m, preferred_element_type=F32) + jnp.dot(lo, m, preferred_element_type=F32))


def _inproj_kernel(x_ref, gmix_ref, w_ref, qg_ref, kg_ref, cos_ref, sin_ref, bavg_ref,
                   qa_ref, ka_ref, va_ref, qb_ref, kb_ref, vb_ref, sga_ref, sgb_ref,
                   ka16_ref, va16_ref, kb16_ref, vb16_ref, kmean_ref, *, kv_tokens_on_lanes):
    kv_out = (lambda a: a.T) if kv_tokens_on_lanes else (lambda a: a)
    x = x_ref[...]
    ms = jnp.mean(x * x, axis=-1, keepdims=True)
    xn = ((x * lax.rsqrt(ms + RMS_EPS)) * gmix_ref[...]).astype(BF16)

    def proj(c0, c1):
        return jnp.dot(xn, w_ref[:, c0:c1], preferred_element_type=F32)

    cos = jnp.concatenate([cos_ref[...]] * (W_MIX // LANES), axis=1)
    sin = jnp.concatenate([sin_ref[...]] * (W_MIX // LANES), axis=1)
    lane = lax.broadcasted_iota(jnp.int32, (x.shape[0], W_MIX), 1)
    low_half = (lane % HEAD_DIM) < (HEAD_DIM // 2)
    bavg = bavg_ref[...]

    def headnorm_rope(z, gain):
        msq = _split_dot(z * z, bavg)
        y = (z * lax.rsqrt(msq + RMS_EPS)) * gain
        partner = jnp.where(low_half, pltpu.roll(y, W_MIX - HEAD_DIM // 2, 1),
                            pltpu.roll(y, HEAD_DIM // 2, 1))
        return y * cos + partner * sin

    w = W_MIX
    qa_ref[...] = headnorm_rope(proj(0, w), qg_ref[...])
    ka = headnorm_rope(proj(w, 2 * w), kg_ref[...])
    ka_ref[...] = kv_out(ka)
    ka16_ref[...] = ka.astype(BF16)
    nblk = ka.shape[0] // MOBA_BLOCK
    kmean_ref[0] = jnp.mean(ka.reshape(nblk, MOBA_BLOCK, W_MIX), axis=1)
    va = proj(2 * w, 3 * w)
    va_ref[...] = kv_out(va)
    va16_ref[...] = va.astype(BF16)
    qb_ref[...] = proj(3 * w, 4 * w)
    kb = proj(4 * w, 5 * w)
    kb_ref[...] = kv_out(kb)
    kb16_ref[...] = kb.astype(BF16)
    vb = proj(5 * w, 6 * w)
    vb_ref[...] = kv_out(vb)
    vb16_ref[...] = vb.astype(BF16)
    sga_ref[...] = jax.nn.sigmoid(proj(6 * w, 6 * w + D_MODEL))
    sgb_ref[...] = jax.nn.sigmoid(proj(6 * w + D_MODEL, 6 * w + 2 * D_MODEL))


def _rope_tables(pos):
    half = HEAD_DIM // 2
    inv = ROPE_THETA ** (-jnp.arange(half, dtype=F32) / half)
    ang = pos.astype(F32)[:, None] * inv[None, :]
    cos, sin = jnp.cos(ang), jnp.sin(ang)
    reps = LANES // HEAD_DIM
    return (jnp.concatenate([cos, cos] * reps, axis=1), jnp.concatenate([-sin, sin] * reps, axis=1))


def _inproj(x, pos, gmix, w_in16, qgain, kgain, kv_tokens_on_lanes):
    t = x.shape[0]
    assert t % TOK_TILE == 0
    nt = t // TOK_TILE
    cos, sin = _rope_tables(pos)
    head = jnp.arange(W_MIX) // HEAD_DIM
    bavg = jnp.where(head[:, None] == head[None, :], 1.0 / HEAD_DIM, 0.0).astype(BF16)
    tile = lambda wd: pl.BlockSpec((TOK_TILE, wd), lambda i: (i, 0))
    row = lambda wd: pl.BlockSpec((1, wd), lambda i: (0, 0))
    f32_w = jax.ShapeDtypeStruct((t, W_MIX), F32)
    b16_w = jax.ShapeDtypeStruct((t, W_MIX), BF16)
    f32_d = jax.ShapeDtypeStruct((t, D_MODEL), F32)
    nblk = TOK_TILE // MOBA_BLOCK
    if kv_tokens_on_lanes:
        kv_spec = pl.BlockSpec((W_MIX, TOK_TILE), lambda i: (0, i))
        kv_shape = jax.ShapeDtypeStruct((W_MIX, t), F32)
    else:
        kv_spec, kv_shape = tile(W_MIX), f32_w
    return pl.pallas_call(
        functools.partial(_inproj_kernel, kv_tokens_on_lanes=kv_tokens_on_lanes),
        grid=(nt,),
        in_specs=[tile(D_MODEL), row(D_MODEL), _resident(w_in16.shape), row(W_MIX), row(W_MIX),
                  tile(LANES), tile(LANES), _resident((W_MIX, W_MIX))],
        out_specs=[tile(W_MIX), kv_spec, kv_spec] * 2 + [tile(D_MODEL)] * 2 + [tile(W_MIX)] * 4
                  + [pl.BlockSpec((1, nblk, W_MIX), lambda i: (i, 0, 0))],
        out_shape=[f32_w, kv_shape, kv_shape] * 2 + [f32_d] * 2 + [b16_w] * 4
                  + [jax.ShapeDtypeStruct((nt, nblk, W_MIX), F32)],
        compiler_params=_cparams(("parallel",)),
        name="inproj",
    )(x, gmix.reshape(1, -1), w_in16, jnp.tile(qgain, N_HEADS).reshape(1, -1),
      jnp.tile(kgain, N_HEADS).reshape(1, -1), cos, sin, bavg)


def _top_mask_lanes(g, count):
    lane = lax.broadcasted_iota(jnp.int32, g.shape, 1)
    sel = jnp.zeros(g.shape, jnp.bool_)
    for _ in range(count):
        m = jnp.max(g, axis=1, keepdims=True)
        idx = jnp.min(jnp.where(g == m, lane, g.shape[1]), axis=1, keepdims=True)
        pick = (lane == idx) & (m > NEG_INF / 2)
        sel = sel | pick
        g = jnp.where(pick, NEG_INF, g)
    return sel


def _moba_prompt_kernel(q_ref, k_ref, v_ref, gmat_ref, o_ref):
    i = pl.program_id(0)
    tq = Q_TILE
    lane = lax.broadcasted_iota(jnp.int32, (tq, LANES), 1)
    row = lax.broadcasted_iota(jnp.int32, (tq, tq), 0)
    col = lax.broadcasted_iota(jnp.int32, (tq, tq), 1)
    klane = lax.broadcasted_iota(jnp.int32, (MOBA_BLOCK, LANES), 1)
    own0 = pl.multiple_of(i * MOBA_BLOCK, MOBA_BLOCK)
    for p in range(N_HEADS // 2):
        c0 = p * LANES
        q2 = q_ref[:, c0:c0 + LANES]
        outs = []
        for w in range(2):
            mine = (lane // HEAD_DIM) == w
            kmine = (klane // HEAD_DIM) == w
            nlane = lane - (1 - w) * HEAD_DIM
            knlane = klane - (1 - w) * HEAD_DIM
            gate = jnp.dot(jnp.where(mine, q2, 0.0), gmat_ref[p, w], preferred_element_type=F32,
                           precision=lax.Precision.HIGHEST)
            past = (~mine) & (nlane < i)
            sel = _top_mask_lanes(jnp.where(past, gate, NEG_INF), MOBA_TOPK)
            qe = jnp.where(mine, q2 * (1.0 / math.sqrt(HEAD_DIM)), jnp.where(sel, 0.0, NEG_INF)).astype(BF16)

            k_own = jnp.where(kmine, k_ref[pl.ds(own0, MOBA_BLOCK), c0:c0 + LANES], 0.0)
            s = lax.dot_general(qe, k_own, (((1,), (1,)), ((), ())), preferred_element_type=F32)
            s = jnp.where(col <= row, s, NEG_INF)
            m0 = jnp.max(s, axis=1, keepdims=True)
            p0 = jnp.exp(s - m0)
            l0 = jnp.sum(p0, axis=1, keepdims=True)
            a0 = jnp.dot(p0.astype(BF16), v_ref[pl.ds(own0, MOBA_BLOCK), c0:c0 + LANES],
                         preferred_element_type=F32)

            def body(n, carry, qe=qe, kmine=kmine, knlane=knlane, c0=c0):
                m, l, acc = carry
                r0 = pl.multiple_of(n * MOBA_BLOCK, MOBA_BLOCK)
                kn = k_ref[pl.ds(r0, MOBA_BLOCK), c0:c0 + LANES]
                ke = jnp.where(kmine, kn, jnp.where(knlane == n, 1.0, 0.0).astype(BF16))
                s = lax.dot_general(qe, ke, (((1,), (1,)), ((), ())), preferred_element_type=F32)
                m_new = jnp.maximum(m, jnp.max(s, axis=1, keepdims=True))
                alpha = jnp.exp(m - m_new)
                pn = jnp.exp(s - m_new)
                l = alpha * l + jnp.sum(pn, axis=1, keepdims=True)
                acc = alpha * acc + jnp.dot(pn.astype(BF16), v_ref[pl.ds(r0, MOBA_BLOCK), c0:c0 + LANES],
                                            preferred_element_type=F32)
                return m_new, l, acc

            _, l, acc = lax.fori_loop(0, i, body, (m0, l0, a0))
            outs.append(acc / l)
        o_ref[:, c0:c0 + LANES] = jnp.where(lane < HEAD_DIM, outs[0], outs[1])


def _moba_gate_matrices(kmean):
    nb = kmean.shape[0]
    assert nb <= HEAD_DIM
    kt = jnp.pad(kmean, ((0, HEAD_DIM - nb), (0, 0))).T.reshape(N_HEADS // 2, 2, HEAD_DIM, HEAD_DIM)
    z = jnp.zeros_like(kt[:, 0])
    g0 = jnp.concatenate([jnp.concatenate([z, kt[:, 0]], axis=2), jnp.concatenate([z, z], axis=2)], axis=1)
    g1 = jnp.concatenate([jnp.concatenate([z, z], axis=2), jnp.concatenate([kt[:, 1], z], axis=2)], axis=1)
    return jnp.stack([g0, g1], axis=1)


def _moba_prompt(qa, ka16, va16, kmean):
    s = qa.shape[0]
    assert s % Q_TILE == 0 and Q_TILE == MOBA_BLOCK
    gmat = _moba_gate_matrices(kmean)
    return pl.pallas_call(
        _moba_prompt_kernel,
        grid=(s // Q_TILE,),
        in_specs=[pl.BlockSpec((Q_TILE, W_MIX), lambda i: (i, 0)), _resident(ka16.shape), _resident(va16.shape),
                  _resident(gmat.shape)],
        out_specs=pl.BlockSpec((Q_TILE, W_MIX), lambda i: (i, 0)),
        out_shape=jax.ShapeDtypeStruct((s, W_MIX), F32),
        compiler_params=_cparams(("parallel",)),
        name="moba_prompt",
    )(qa, ka16, va16, gmat)


def _sb_tile(z, mask, carry, tri):
    t = jnp.log1p(jnp.exp(-jnp.abs(z)))
    log_beta = jnp.minimum(z, 0.0) - t
    log_keep = jnp.where(mask, -jnp.maximum(z, 0.0) - t, 0.0)
    after = _split_dot(log_keep, tri) + carry
    a = jnp.where(mask, jnp.exp(log_beta + after), 0.0)
    return a, carry + jnp.sum(log_keep, axis=1, keepdims=True)


def _sb_prompt_kernel(q_ref, k_ref, v_ref, tri_ref, o_ref):
    i = pl.program_id(0)
    tq = Q_TILE
    lane = lax.broadcasted_iota(jnp.int32, (tq, LANES), 1)
    row = lax.broadcasted_iota(jnp.int32, (tq, tq), 0)
    col = lax.broadcasted_iota(jnp.int32, (tq, tq), 1)
    tri = tri_ref[...]
    for p in range(N_HEADS // 2):
        c0 = p * LANES
        q2 = q_ref[:, c0:c0 + LANES]
        outs = []
        for w in range(2):
            mine = (lane // HEAD_DIM) == w
            qe = jnp.where(mine, q2 * (1.0 / math.sqrt(HEAD_DIM)), 0.0).astype(BF16)

            def cond(state):
                j, carry, _ = state
                return (j >= 0) & (jnp.max(carry) > SB_UNDERFLOW)

            def body(state, qe=qe, c0=c0):
                j, carry, acc = state
                r0 = pl.multiple_of(j * tq, tq)
                z = lax.dot_general(qe, k_ref[pl.ds(r0, tq), c0:c0 + LANES], (((1,), (1,)), ((), ())),
                                    preferred_element_type=F32)
                mask = (col + (j - i) * tq) < row
                a, carry = _sb_tile(z, mask, carry, tri)
                acc = acc + jnp.dot(a.astype(BF16), v_ref[pl.ds(r0, tq), c0:c0 + LANES],
                                    preferred_element_type=F32)
                return j - 1, carry, acc

            init = (i, jnp.zeros((tq, 1), F32), jnp.zeros((tq, LANES), F32))
            outs.append(lax.while_loop(cond, body, init)[2])
        o_ref[:, c0:c0 + LANES] = jnp.where(lane < HEAD_DIM, outs[0], outs[1])


def _tri(n):
    r = jnp.arange(n)
    return (r[:, None] > r[None, :]).astype(BF16)


def _sb_prompt(qb, kb16, vb16):
    s = qb.shape[0]
    assert s % Q_TILE == 0
    return pl.pallas_call(
        _sb_prompt_kernel,
        grid=(s // Q_TILE,),
        in_specs=[pl.BlockSpec((Q_TILE, W_MIX), lambda i: (i, 0)), _resident(kb16.shape), _resident(vb16.shape),
                  _resident((Q_TILE, Q_TILE))],
        out_specs=pl.BlockSpec((Q_TILE, W_MIX), lambda i: (i, 0)),
        out_shape=jax.ShapeDtypeStruct((s, W_MIX), F32),
        compiler_params=_cparams(("parallel",)),
        name="sb_prompt",
    )(qb, kb16, vb16, _tri(Q_TILE))


PAGES_PER_STEP = 8
ROWS = LANES


def _head_block_diag(q):
    t = q.shape[0]
    lanehead = lax.broadcasted_iota(jnp.int32, q.shape, 1) // HEAD_DIM
    parts = [jnp.where(lanehead == h, q, 0.0) for h in range(N_HEADS)]
    parts.append(jnp.zeros((ROWS - N_HEADS * t, q.shape[1]), F32))
    return jnp.concatenate(parts, axis=0)


def _head_diag_extract(full, t):
    lanehead = lax.broadcasted_iota(jnp.int32, (t, full.shape[1]), 1) // HEAD_DIM
    out = jnp.zeros((t, full.shape[1]), F32)
    for h in range(N_HEADS):
        out = out + jnp.where(lanehead == h, full[h * t:(h + 1) * t], 0.0)
    return out


def _pad_rows(x):
    return jnp.concatenate([x, jnp.zeros((ROWS - x.shape[0], x.shape[1]), x.dtype)], axis=0)


def _moba_sample_kernel(pt_ref, q_ref, knew_ref, vnew_ref, *rest, n_blocks, t_new):
    kp = rest[:PAGES_PER_STEP]
    vp = rest[PAGES_PER_STEP:2 * PAGES_PER_STEP]
    o_ref, qbd_sc, m_sc, l_sc, acc_sc, kmean_sc = rest[2 * PAGES_PER_STEP:]
    g = pl.program_id(1)
    lane = lax.broadcasted_iota(jnp.int32, (ROWS, LANES), 1)
    row = lax.broadcasted_iota(jnp.int32, (ROWS, LANES), 0)

    @pl.when(g == 0)
    def _():
        qbd_sc[...] = _head_block_diag(q_ref[0])
        m_sc[...] = jnp.zeros_like(m_sc)
        l_sc[...] = jnp.zeros_like(l_sc)
        kmean_sc[...] = jnp.zeros_like(kmean_sc)

    qbd = qbd_sc[...]
    qs = (qbd * (1.0 / math.sqrt(HEAD_DIM))).astype(BF16)
    pages_per_block = MOBA_BLOCK // PAGE_SIZE
    for jj in range(PAGES_PER_STEP // pages_per_block):
        n = g * (PAGES_PER_STEP // pages_per_block) + jj
        kt = jnp.concatenate([kp[jj * pages_per_block + r][...] for r in range(pages_per_block)], axis=1)
        vt = jnp.concatenate([vp[jj * pages_per_block + r][...] for r in range(pages_per_block)], axis=1)
        s = jnp.dot(qs, kt.astype(BF16), preferred_element_type=F32)
        m = jnp.max(s, axis=1, keepdims=True)
        p = jnp.exp(s - m)
        m_sc[...] = jnp.where(lane == n, m, m_sc[...])
        l_sc[...] = jnp.where(lane == n, jnp.sum(p, axis=1, keepdims=True), l_sc[...])
        acc_sc[n] = lax.dot_general(p.astype(BF16), vt.astype(BF16), (((1,), (1,)), ((), ())),
                                    preferred_element_type=F32)
        klane = lax.broadcasted_iota(jnp.int32, kmean_sc.shape, 1)
        kmean_sc[...] = jnp.where(klane == n, jnp.mean(kt, axis=1, keepdims=True), kmean_sc[...])

    @pl.when(g == pl.num_programs(1) - 1)
    def _():
        gate = jnp.dot(qbd, kmean_sc[...], preferred_element_type=F32, precision=lax.Precision.HIGHEST)
        sel = _top_mask_lanes(jnp.where(lane < n_blocks, gate, NEG_INF), MOBA_TOPK)
        m_all = m_sc[...]
        s_own = lax.dot_general(qs, _pad_rows(knew_ref[0]).astype(BF16), (((1,), (1,)), ((), ())),
                                preferred_element_type=F32)
        s_own = jnp.where((lane <= row % t_new) & (lane < t_new), s_own, NEG_INF)
        m_own = jnp.max(s_own, axis=1, keepdims=True)
        p_own = jnp.exp(s_own - m_own)
        m_tot = jnp.maximum(jnp.max(jnp.where(sel, m_all, NEG_INF), axis=1, keepdims=True), m_own)
        wgt = jnp.where(sel, jnp.exp(m_all - m_tot), 0.0)
        w_own = jnp.exp(m_own - m_tot)
        l_tot = jnp.sum(wgt * l_sc[...], axis=1, keepdims=True) + w_own * jnp.sum(p_own, axis=1, keepdims=True)
        out = w_own * jnp.dot(p_own.astype(BF16), _pad_rows(vnew_ref[0]).astype(BF16), preferred_element_type=F32)
        for nb in range(n_blocks):
            out = out + wgt[:, nb:nb + 1] * acc_sc[nb]
        o_ref[0] = _head_diag_extract(out / l_tot, t_new)


def _page_specs(layer):
    def spec(r):
        return pl.BlockSpec((None, None, W_MIX, PAGE_SIZE),
                            lambda b, g, pt: (layer, pt[b, g * PAGES_PER_STEP + r], 0, 0))
    return [spec(r) for r in range(PAGES_PER_STEP)]


def _moba_sample(page_table, qa, knew, vnew, cache_k, cache_v, layer):
    nseq, t_new, _ = qa.shape
    n_pages = page_table.shape[1]
    past = n_pages * PAGE_SIZE
    assert past % MOBA_BLOCK == 0 and t_new <= MOBA_BLOCK and t_new % 8 == 0 and N_HEADS * t_new <= ROWS
    assert n_pages % PAGES_PER_STEP == 0 and past // MOBA_BLOCK <= LANES
    n_blocks = past // MOBA_BLOCK
    new_spec = pl.BlockSpec((1, t_new, W_MIX), lambda b, g, pt: (b, 0, 0))
    grid_spec = pltpu.PrefetchScalarGridSpec(
        num_scalar_prefetch=1,
        grid=(nseq, n_pages // PAGES_PER_STEP),
        in_specs=[new_spec] * 3 + _page_specs(layer) + _page_specs(layer),
        out_specs=new_spec,
        scratch_shapes=[pltpu.VMEM((ROWS, W_MIX), F32), pltpu.VMEM((ROWS, LANES), F32),
                        pltpu.VMEM((ROWS, LANES), F32), pltpu.VMEM((n_blocks, ROWS, W_MIX), F32),
                        pltpu.VMEM((W_MIX, LANES), F32)],
    )
    return pl.pallas_call(
        functools.partial(_moba_sample_kernel, n_blocks=n_blocks, t_new=t_new),
        grid_spec=grid_spec,
        out_shape=jax.ShapeDtypeStruct((nseq, t_new, W_MIX), F32),
        compiler_params=_cparams(("parallel", "arbitrary")),
        name="moba_sample",
    )(page_table, qa, knew, vnew, *([cache_k] * PAGES_PER_STEP), *([cache_v] * PAGES_PER_STEP))


SB_EAGER_PAGES = 2


def _sb_sample_kernel(pt_ref, q_ref, knew_ref, vnew_ref, k1_ref, k2_ref, v1_ref, v2_ref, kc_ref, vc_ref, tri_ref,
                      o_ref, kbuf, vbuf, sem, *, layer, n_pages, t_new):
    b = pl.program_id(0)
    lane = lax.broadcasted_iota(jnp.int32, (ROWS, LANES), 1)
    row = lax.broadcasted_iota(jnp.int32, (ROWS, LANES), 0)
    real_row = lax.broadcasted_iota(jnp.int32, (ROWS, 1), 0) < N_HEADS * t_new
    tri = tri_ref[...]
    qs = (_head_block_diag(q_ref[0]) * (1.0 / math.sqrt(HEAD_DIM))).astype(BF16)

    nt = (((1,), (1,)), ((), ()))

    def tile(kt, vt, carry, acc):
        z = jnp.dot(qs, kt.astype(BF16), preferred_element_type=F32)
        a, carry = _sb_tile(z, every, carry, tri)
        return carry, acc + lax.dot_general(a.astype(BF16), vt.astype(BF16), nt, preferred_element_type=F32)

    every = lane >= 0
    new_mask = (lane < row % t_new) & (lane < t_new)
    z = lax.dot_general(qs, _pad_rows(knew_ref[0]).astype(BF16), nt, preferred_element_type=F32)
    a, carry = _sb_tile(z, new_mask, jnp.zeros((ROWS, 1), F32), tri)
    acc = jnp.dot(a.astype(BF16), _pad_rows(vnew_ref[0]).astype(BF16), preferred_element_type=F32)
    carry, acc = tile(k1_ref[...], v1_ref[...], carry, acc)
    carry, acc = tile(k2_ref[...], v2_ref[...], carry, acc)

    def cond(state):
        p, carry, _ = state
        return (p >= 0) & (jnp.max(jnp.where(real_row, carry, NEG_INF)) > SB_UNDERFLOW)

    def body(state):
        p, carry, acc = state
        page = pt_ref[b, p]
        ck = pltpu.make_async_copy(kc_ref.at[layer, page], kbuf, sem.at[0])
        cv = pltpu.make_async_copy(vc_ref.at[layer, page], vbuf, sem.at[1])
        ck.start()
        cv.start()
        ck.wait()
        cv.wait()
        carry, acc = tile(kbuf[...], vbuf[...], carry, acc)
        return p - 1, carry, acc

    _, _, acc = lax.while_loop(cond, body, (jnp.int32(n_pages - SB_EAGER_PAGES - 1), carry, acc))
    o_ref[0] = _head_diag_extract(acc, t_new)


def _sb_sample(page_table, qb, knew, vnew, cache_k, cache_v, layer):
    nseq, t_new, _ = qb.shape
    n_pages = page_table.shape[1]
    assert n_pages >= SB_EAGER_PAGES and t_new % 8 == 0 and N_HEADS * t_new <= ROWS and t_new <= LANES
    new_spec = pl.BlockSpec((1, t_new, W_MIX), lambda b, pt: (b, 0, 0))
    page = lambda back: pl.BlockSpec((None, None, W_MIX, PAGE_SIZE),
                                     lambda b, pt: (layer, pt[b, n_pages - back], 0, 0))
    grid_spec = pltpu.PrefetchScalarGridSpec(
        num_scalar_prefetch=1,
        grid=(nseq,),
        in_specs=[new_spec] * 3 + [page(1), page(2), page(1), page(2),
                                   pl.BlockSpec(memory_space=pl.ANY), pl.BlockSpec(memory_space=pl.ANY),
                                   pl.BlockSpec((PAGE_SIZE, PAGE_SIZE), lambda b, pt: (0, 0))],
        out_specs=new_spec,
        scratch_shapes=[pltpu.VMEM((W_MIX, PAGE_SIZE), F32), pltpu.VMEM((W_MIX, PAGE_SIZE), F32),
                        pltpu.SemaphoreType.DMA((2,))],
    )
    return pl.pallas_call(
        functools.partial(_sb_sample_kernel, layer=layer, n_pages=n_pages, t_new=t_new),
        grid_spec=grid_spec,
        out_shape=jax.ShapeDtypeStruct((nseq, t_new, W_MIX), F32),
        compiler_params=_cparams(("parallel",)),
        name="sb_sample",
    )(page_table, qb, knew, vnew, cache_k, cache_k, cache_v, cache_v, cache_k, cache_v, _tri(PAGE_SIZE))


PEER_HALF = 128


def _postmix_kernel(x_ref, oa_ref, ob_ref, sga_ref, sgb_ref, wua_ref, wub_ref, wo_ref, gffn_ref, wpqt_ref, sk_ref,
                    h_ref, hnt_ref, s1t_ref, s2t_ref):
    ya = jnp.dot(oa_ref[...].astype(BF16), wua_ref[...], preferred_element_type=F32)
    yb = jnp.dot(ob_ref[...].astype(BF16), wub_ref[...], preferred_element_type=F32)
    merged = sga_ref[...] * ya + sgb_ref[...] * yb
    h = x_ref[...] + jnp.dot(merged.astype(BF16), wo_ref[...], preferred_element_type=F32)
    h_ref[...] = h
    ms = jnp.mean(h * h, axis=-1, keepdims=True)
    hn = (h * lax.rsqrt(ms + RMS_EPS)) * gffn_ref[...]
    hnt = hn.T.astype(BF16)
    hnt_ref[...] = hnt
    pqt = jnp.dot(wpqt_ref[...], hnt, preferred_element_type=F32)
    for hh in range(PEER_HEADS):
        r0 = hh * 2 * PEER_HALF
        s1t_ref[hh] = jnp.dot(sk_ref[hh, 0], pqt[r0:r0 + PEER_HALF].astype(BF16), preferred_element_type=F32)
        s2t_ref[hh] = jnp.dot(sk_ref[hh, 1], pqt[r0 + PEER_HALF:r0 + 2 * PEER_HALF].astype(BF16),
                              preferred_element_type=F32)


def _postmix(x, oa, ob, sga, sgb, wua16, wub16, wo16, gffn, wpqt16, sk16):
    t = x.shape[0]
    assert t % TOK_TILE == 0
    tile = lambda wd: pl.BlockSpec((TOK_TILE, wd), lambda i: (i, 0))
    score = pl.BlockSpec((PEER_HEADS, PEER_NKEYS, TOK_TILE), lambda i: (0, 0, i))
    return pl.pallas_call(
        _postmix_kernel,
        grid=(t // TOK_TILE,),
        in_specs=[tile(D_MODEL), tile(W_MIX), tile(W_MIX), tile(D_MODEL), tile(D_MODEL),
                  _resident(wua16.shape), _resident(wub16.shape), _resident(wo16.shape),
                  pl.BlockSpec((1, D_MODEL), lambda i: (0, 0)), _resident(wpqt16.shape), _resident(sk16.shape)],
        out_specs=[tile(D_MODEL), pl.BlockSpec((D_MODEL, TOK_TILE), lambda i: (0, i)), score, score],
        out_shape=[jax.ShapeDtypeStruct((t, D_MODEL), F32), jax.ShapeDtypeStruct((D_MODEL, t), BF16),
                   jax.ShapeDtypeStruct((PEER_HEADS, PEER_NKEYS, t), F32),
                   jax.ShapeDtypeStruct((PEER_HEADS, PEER_NKEYS, t), F32)],
        compiler_params=_cparams(("parallel",)),
        name="postmix",
    )(x, oa, ob, sga, sgb, wua16, wub16, wo16, gffn.reshape(1, -1), wpqt16, sk16)


ROUTE_LANES = 512
NOT_RANKED = 127.0


def _top_rows(s, k):
    n, l = s.shape
    riota = lax.broadcasted_iota(jnp.int32, (n, l), 0)
    kiota = lax.broadcasted_iota(jnp.int32, (k, l), 0)

    def body(r, carry):
        s, rank, vals = carry
        m = jnp.max(s, axis=0, keepdims=True)
        idx = jnp.min(jnp.where(s == m, riota, n), axis=0, keepdims=True)
        pick = riota == idx
        return jnp.where(pick, -jnp.inf, s), jnp.where(pick, r, rank), jnp.where(kiota == r, m, vals)

    _, rank, vals = lax.fori_loop(0, k, body, (s, jnp.full((n, l), k, jnp.int32), jnp.zeros((k, l), F32)))
    return rank, vals


def _route_kernel(s1_ref, s2_ref, na_ref, wa_ref, rb_ref, e2_ref):
    k = PEER_TOPK

    def chunk(c, _):
        sl = pl.ds(pl.multiple_of(c * LANES, LANES), LANES)
        s1 = s1_ref[0, :, sl]
        s2 = s2_ref[0, :, sl]
        rank1, v1 = _top_rows(s1, k)
        rank2, v2 = _top_rows(s2, k)
        cand = jnp.concatenate([v1[i:i + 1] + v2 for i in range(k)], axis=0)
        n, l = cand.shape
        riota = lax.broadcasted_iota(jnp.int32, (n, l), 0)
        kiota = lax.broadcasted_iota(jnp.int32, (k, l), 0)
        top = cand[0:1]

        def body(r, carry):
            cnd, cnt, z = carry
            m = jnp.max(cnd, axis=0, keepdims=True)
            idx = jnp.min(jnp.where(cnd == m, riota, n), axis=0, keepdims=True)
            cnt = jnp.where(kiota == idx // k, cnt + 1.0, cnt)
            return jnp.where(riota == idx, -jnp.inf, cnd), cnt, z + jnp.exp(m - top)

        _, cnt, z = lax.fori_loop(0, k, body, (cand, jnp.zeros((k, l), F32), jnp.zeros((1, l), F32)))
        na = jnp.zeros(s1.shape, F32)
        for i in range(k):
            na = jnp.where(rank1 == i, cnt[i:i + 1], na)
        na_ref[0, :, sl] = na
        wa_ref[0, :, sl] = jnp.where(rank1 < k, jnp.exp(s1 - v1[0:1]) / z, 0.0)
        rb_ref[0, :, sl] = jnp.where(rank2 < k, rank2.astype(F32), NOT_RANKED)
        e2_ref[0, :, sl] = jnp.where(rank2 < k, jnp.exp(s2 - v2[0:1]), 0.0)
        return 0

    lax.fori_loop(0, ROUTE_LANES // LANES, chunk, 0)


def _route(s1t, s2t):
    t = s1t.shape[2]
    assert t % ROUTE_LANES == 0
    blk = pl.BlockSpec((1, PEER_NKEYS, ROUTE_LANES), lambda i, h: (h, 0, i))
    shp = jax.ShapeDtypeStruct(s1t.shape, F32)
    return pl.pallas_call(
        _route_kernel,
        grid=(t // ROUTE_LANES, PEER_HEADS),
        in_specs=[blk, blk],
        out_specs=[blk] * 4,
        out_shape=[shp] * 4,
        compiler_params=_cparams(("parallel", "parallel")),
        name="peer_route",
    )(s1t, s2t)


PEER_TOK = 512
A_PER_STEP = 4


def _peer_kernel(hnt_ref, na_ref, wa_ref, rb_ref, e2_ref, u_ref, v_ref, o_ref, acc_sc):
    j = pl.program_id(1)

    @pl.when(j == 0)
    def _():
        acc_sc[...] = jnp.zeros_like(acc_sc)

    hnt = hnt_ref[...]
    parts = []
    for aa in range(A_PER_STEP):
        a = j * A_PER_STEP + aa
        act = jnp.dot(u_ref[aa * PEER_NKEYS:(aa + 1) * PEER_NKEYS, :], hnt, preferred_element_type=F32)
        gel = 0.5 * act * (1.0 + lax.erf(act * (1.0 / math.sqrt(2.0))))
        gate = jnp.zeros(act.shape, F32)
        for h in range(PEER_HEADS):
            na = na_ref[h, pl.ds(a, 1), :]
            wa = wa_ref[h, pl.ds(a, 1), :]
            gate = gate + jnp.where(rb_ref[h] < na, e2_ref[h], 0.0) * wa
        parts.append((gate * gel).astype(BF16))
    pt = jnp.concatenate(parts, axis=0)
    acc_sc[...] += lax.dot_general(pt, v_ref[...], (((0,), (0,)), ((), ())), preferred_element_type=F32)

    @pl.when(j == pl.num_programs(1) - 1)
    def _():
        o_ref[...] = acc_sc[...]


def _peer(hnt, na, wa, rb, e2, u16, v16):
    t = hnt.shape[1]
    assert t % PEER_TOK == 0 and u16.shape[0] == PEER_NKEYS * PEER_NKEYS
    n_exp = A_PER_STEP * PEER_NKEYS
    route = pl.BlockSpec((PEER_HEADS, PEER_NKEYS, PEER_TOK), lambda i, j: (0, 0, i))
    table = pl.BlockSpec((n_exp, D_MODEL), lambda i, j: (j, 0))
    return pl.pallas_call(
        _peer_kernel,
        grid=(t // PEER_TOK, PEER_NKEYS // A_PER_STEP),
        in_specs=[pl.BlockSpec((D_MODEL, PEER_TOK), lambda i, j: (0, i)), route, route, route, route, table, table],
        out_specs=pl.BlockSpec((PEER_TOK, D_MODEL), lambda i, j: (i, 0)),
        out_shape=jax.ShapeDtypeStruct((t, D_MODEL), F32),
        scratch_shapes=[pltpu.VMEM((PEER_TOK, D_MODEL), F32)],
        compiler_params=_cparams(("parallel", "arbitrary")),
        name="peer_experts",
    )(hnt, na, wa, rb, e2, u16, v16)


def _ple_kernel(h_ref, f_ref, p_ref, gple_ref, wg_ref, wp_ref, y_ref):
    h = h_ref[...] + f_ref[...]
    ms = jnp.mean(h * h, axis=-1, keepdims=True)
    hn = ((h * lax.rsqrt(ms + RMS_EPS)) * gple_ref[...]).astype(BF16)
    gate = jax.nn.sigmoid(jnp.dot(hn, wg_ref[...], preferred_element_type=F32))
    y_ref[...] = h + gate * jnp.dot(p_ref[...].astype(BF16), wp_ref[...], preferred_element_type=F32)


def _ple(h, ffn, p, gple, wg16, wp16):
    t = h.shape[0]
    assert t % TOK_TILE == 0
    tile = lambda wd: pl.BlockSpec((TOK_TILE, wd), lambda i: (i, 0))
    return pl.pallas_call(
        _ple_kernel,
        grid=(t // TOK_TILE,),
        in_specs=[tile(D_MODEL), tile(D_MODEL), tile(p.shape[1]), pl.BlockSpec((1, D_MODEL), lambda i: (0, 0)),
                  _resident(wg16.shape), _resident(wp16.shape)],
        out_specs=tile(D_MODEL),
        out_shape=jax.ShapeDtypeStruct((t, D_MODEL), F32),
        compiler_params=_cparams(("parallel",)),
        name="ple_gate",
    )(h, ffn, p, gple.reshape(1, -1), wg16, wp16)


def _post_mix(x, oa, ob, sga, sgb, p, wts):
    h, hnt, s1t, s2t = _postmix(x, oa, ob, sga, sgb, wts["wua"], wts["wub"], wts["wo"], wts["gffn"],
                                wts["wpqt"], wts["sk"])
    na, wa, rb, e2 = _route(s1t, s2t)
    ffn = _peer(hnt, na, wa, rb, e2, wts["u"], wts["v"])
    return _ple(h, ffn, p, wts["gple"], wts["wg"], wts["wp"])


def kernel(x_prompt, x_sample, cache_k_moba, cache_v_moba, cache_k_sb, cache_v_sb, page_table, p_prompt, p_sample,
           norm_mix, w_in, q_norm_moba, k_norm_moba, w_up_moba, w_up_sb, w_o, norm_ffn, w_peer_q, peer_sub_keys,
           peer_u, peer_v, norm_ple, w_ple_gate, w_ple):
    depth = w_in.shape[0]
    nb, s, _ = x_prompt.shape
    nseq, t_new, _ = x_sample.shape
    past = page_table.shape[1] * PAGE_SIZE
    pos_p = jnp.arange(s, dtype=jnp.int32)
    pos_s = jnp.tile(past + jnp.arange(t_new, dtype=jnp.int32), nseq)
    paged = lambda c: jnp.transpose(c, (0, 1, 3, 4, 2)).reshape(c.shape[0], c.shape[1], W_MIX, PAGE_SIZE)
    ckm, cvm, cks, cvs = paged(cache_k_moba), paged(cache_v_moba), paged(cache_k_sb), paged(cache_v_sb)
    heads = lambda a, lead: a.reshape(*lead, N_HEADS, HEAD_DIM)
    heads_t = lambda a: jnp.transpose(a.reshape(N_HEADS, HEAD_DIM, -1), (2, 0, 1))

    hp = [x_prompt[b] for b in range(nb)]
    hs = x_sample.reshape(nseq * t_new, D_MODEL)
    kv_p = [[] for _ in range(4)]
    kv_s = [[] for _ in range(4)]
    for l in range(depth):
        w_in16 = w_in[l].astype(BF16)
        wts = dict(wua=w_up_moba[l].astype(BF16), wub=w_up_sb[l].astype(BF16), wo=w_o[l].astype(BF16),
                   gffn=norm_ffn[l], wpqt=w_peer_q[l].T.astype(BF16), sk=peer_sub_keys[l].astype(BF16),
                   u=peer_u[l].astype(BF16), v=peer_v[l].astype(BF16), gple=norm_ple[l],
                   wg=w_ple_gate[l].astype(BF16), wp=w_ple[l].astype(BF16))
        layer_kv = [[] for _ in range(4)]
        for b in range(nb):
            (qa, ka, va, qb, kb, vb, sga, sgb, ka16, va16, kb16, vb16, kmean) = _inproj(
                hp[b], pos_p, norm_mix[l], w_in16, q_norm_moba[l], k_norm_moba[l], True)
            oa = _moba_prompt(qa, ka16, va16, kmean.reshape(-1, W_MIX))
            ob = _sb_prompt(qb, kb16, vb16)
            hp[b] = _post_mix(hp[b], oa, ob, sga, sgb, p_prompt[l, b], wts)
            for dst, a in zip(layer_kv, (ka, va, kb, vb)):
                dst.append(heads_t(a))
        for dst, src in zip(kv_p, layer_kv):
            dst.append(jnp.stack(src))

        (qa, ka, va, qb, kb, vb, sga, sgb, *_) = _inproj(hs, pos_s, norm_mix[l], w_in16, q_norm_moba[l],
                                                          k_norm_moba[l], False)
        seq = lambda a: a.reshape(nseq, t_new, W_MIX)
        oa = _moba_sample(page_table, seq(qa), seq(ka), seq(va), ckm, cvm, l)
        ob = _sb_sample(page_table, seq(qb), seq(kb), seq(vb), cks, cvs, l)
        hs = _post_mix(hs, oa.reshape(-1, W_MIX), ob.reshape(-1, W_MIX), sga, sgb,
                       p_sample[l].reshape(-1, p_sample.shape[-1]), wts)
        for dst, a in zip(kv_s, (ka, va, kb, vb)):
            dst.append(heads(a, (nseq, t_new)))

    return (jnp.stack(hp), hs.reshape(nseq, t_new, D_MODEL),
            *[jnp.stack(a) for a in kv_p], *[jnp.stack(a) for a in kv_s])
```

```python
import functools
import math

import jax
import jax.numpy as jnp
from jax import lax
from jax.experimental import pallas as pl
from jax.experimental.pallas import tpu as pltpu

F32 = jnp.float32
BF16 = jnp.bfloat16

D_MODEL = 1024
HEAD_DIM = 64
N_HEADS = 8
W_MIX = N_HEADS * HEAD_DIM
PAGE_SIZE = 128
MOBA_BLOCK = 256
MOBA_TOPK = 3
ROPE_THETA = 10000.0
PEER_HEADS = 8
PEER_NKEYS = 128
PEER_TOPK = 16
RMS_EPS = 1e-6
NEG_INF = -1e30
LANES = 128
SB_UNDERFLOW = -104.0
VMEM_LIMIT = 56 * 1024 * 1024

TOK_TILE = 512
Q_TILE = 256


def _cparams(sem):
    return pltpu.CompilerParams(dimension_semantics=sem, vmem_limit_bytes=VMEM_LIMIT)


def _resident(shape):
    nd = len(shape)
    return pl.BlockSpec(shape, lambda *_: (0,) * nd, pipeline_mode=pl.Buffered(1))


def _split_dot(x, m):
    hi = x.astype(BF16)
    lo = (x - hi.astype(F32)).astype(BF16)
    return (jnp.dot(hi, m, preferred_element_type=F32) + jnp.dot(lo, m, preferred_element_type=F32))


def _inproj_kernel(x_ref, gmix_ref, w_ref, qg_ref, kg_ref, cos_ref, sin_ref, bavg_ref,
                   qa_ref, ka_ref, va_ref, qb_ref, kb_ref, vb_ref, sga_ref, sgb_ref,
                   ka16_ref, va16_ref, kb16_ref, vb16_ref, kmean_ref, *, kv_tokens_on_lanes):
    kv_out = (lambda a: a.T) if kv_tokens_on_lanes else (lambda a: a)
    x = x_ref[...]
    ms = jnp.mean(x * x, axis=-1, keepdims=True)
    xn = ((x * lax.rsqrt(ms + RMS_EPS)) * gmix_ref[...]).astype(BF16)

    def proj(c0, c1):
        return jnp.dot(xn, w_ref[:, c0:c1], preferred_element_type=F32)

    cos = jnp.concatenate([cos_ref[...]] * (W_MIX // LANES), axis=1)
    sin = jnp.concatenate([sin_ref[...]] * (W_MIX // LANES), axis=1)
    lane = lax.broadcasted_iota(jnp.int32, (x.shape[0], W_MIX), 1)
    low_half = (lane % HEAD_DIM) < (HEAD_DIM // 2)
    bavg = bavg_ref[...]

    def headnorm_rope(z, gain):
        msq = _split_dot(z * z, bavg)
        y = (z * lax.rsqrt(msq + RMS_EPS)) * gain
        partner = jnp.where(low_half, pltpu.roll(y, W_MIX - HEAD_DIM // 2, 1),
                            pltpu.roll(y, HEAD_DIM // 2, 1))
        return y * cos + partner * sin

    w = W_MIX
    qa_ref[...] = headnorm_rope(proj(0, w), qg_ref[...])
    ka = headnorm_rope(proj(w, 2 * w), kg_ref[...])
    ka_ref[...] = kv_out(ka)
    ka16_ref[...] = ka.astype(BF16)
    nblk = ka.shape[0] // MOBA_BLOCK
    kmean_ref[0] = jnp.mean(ka.reshape(nblk, MOBA_BLOCK, W_MIX), axis=1)
    va = proj(2 * w, 3 * w)
    va_ref[...] = kv_out(va)
    va16_ref[...] = kv_out(va).astype(BF16)
    qb_ref[...] = proj(3 * w, 4 * w)
    kb = proj(4 * w, 5 * w)
    kb_ref[...] = kv_out(kb)
    kb16_ref[...] = kb.astype(BF16)
    vb = proj(5 * w, 6 * w)
    vb_ref[...] = kv_out(vb)
    vb16_ref[...] = vb.astype(BF16)
    sga_ref[...] = jax.nn.sigmoid(proj(6 * w, 6 * w + D_MODEL))
    sgb_ref[...] = jax.nn.sigmoid(proj(6 * w + D_MODEL, 6 * w + 2 * D_MODEL))


def _rope_tables(pos):
    half = HEAD_DIM // 2
    inv = ROPE_THETA ** (-jnp.arange(half, dtype=F32) / half)
    ang = pos.astype(F32)[:, None] * inv[None, :]
    cos, sin = jnp.cos(ang), jnp.sin(ang)
    reps = LANES // HEAD_DIM
    return (jnp.concatenate([cos, cos] * reps, axis=1), jnp.concatenate([-sin, sin] * reps, axis=1))


def _inproj(x, pos, gmix, w_in16, qgain, kgain, kv_tokens_on_lanes):
    t = x.shape[0]
    assert t % TOK_TILE == 0
    nt = t // TOK_TILE
    cos, sin = _rope_tables(pos)
    head = jnp.arange(W_MIX) // HEAD_DIM
    bavg = jnp.where(head[:, None] == head[None, :], 1.0 / HEAD_DIM, 0.0).astype(BF16)
    tile = lambda wd: pl.BlockSpec((TOK_TILE, wd), lambda i: (i, 0))
    row = lambda wd: pl.BlockSpec((1, wd), lambda i: (0, 0))
    f32_w = jax.ShapeDtypeStruct((t, W_MIX), F32)
    b16_w = jax.ShapeDtypeStruct((t, W_MIX), BF16)
    f32_d = jax.ShapeDtypeStruct((t, D_MODEL), F32)
    nblk = TOK_TILE // MOBA_BLOCK
    if kv_tokens_on_lanes:
        kv_spec = pl.BlockSpec((W_MIX, TOK_TILE), lambda i: (0, i))
        kv_shape = jax.ShapeDtypeStruct((W_MIX, t), F32)
        va16_shape = jax.ShapeDtypeStruct((W_MIX, t), BF16)
    else:
        kv_spec, kv_shape, va16_shape = tile(W_MIX), f32_w, b16_w
    return pl.pallas_call(
        functools.partial(_inproj_kernel, kv_tokens_on_lanes=kv_tokens_on_lanes),
        grid=(nt,),
        in_specs=[tile(D_MODEL), row(D_MODEL), _resident(w_in16.shape), row(W_MIX), row(W_MIX),
                  tile(LANES), tile(LANES), _resident((W_MIX, W_MIX))],
        out_specs=[tile(W_MIX), kv_spec, kv_spec] * 2 + [tile(D_MODEL)] * 2
                  + [tile(W_MIX), kv_spec, tile(W_MIX), tile(W_MIX)]
                  + [pl.BlockSpec((1, nblk, W_MIX), lambda i: (i, 0, 0))],
        out_shape=[f32_w, kv_shape, kv_shape] * 2 + [f32_d] * 2 + [b16_w, va16_shape, b16_w, b16_w]
                  + [jax.ShapeDtypeStruct((nt, nblk, W_MIX), F32)],
        compiler_params=_cparams(("parallel",)),
        name="inproj",
    )(x, gmix.reshape(1, -1), w_in16, jnp.tile(qgain, N_HEADS).reshape(1, -1),
      jnp.tile(kgain, N_HEADS).reshape(1, -1), cos, sin, bavg)


def _top_mask_lanes(g, count):
    lane = lax.broadcasted_iota(jnp.int32, g.shape, 1)
    sel = jnp.zeros(g.shape, jnp.bool_)
    for _ in range(count):
        m = jnp.max(g, axis=1, keepdims=True)
        idx = jnp.min(jnp.where(g == m, lane, g.shape[1]), axis=1, keepdims=True)
        pick = (lane == idx) & (m > NEG_INF / 2)
        sel = sel | pick
        g = jnp.where(pick, NEG_INF, g)
    return sel


LOG2_E = math.log2(math.e)


def _top_mask_rows(g, count):
    riota = lax.broadcasted_iota(jnp.int32, g.shape, 0)
    sel = jnp.zeros(g.shape, jnp.bool_)
    for _ in range(count):
        m = jnp.max(g, axis=0, keepdims=True)
        idx = jnp.min(jnp.where(g == m, riota, g.shape[0]), axis=0, keepdims=True)
        pick = (riota == idx) & (m > NEG_INF / 2)
        sel = sel | pick
        g = jnp.where(pick, NEG_INF, g)
    return sel


def _moba_prompt_kernel(q_ref, k_ref, vt_ref, kmean_ref, o_ref, wq_sc, sel_sc, m_sc, l_sc, acc_sc, st_sc):
    i = pl.program_id(0)
    tq = Q_TILE
    npairs = N_HEADS // 2
    nbp = kmean_ref.shape[0]
    qt = q_ref[...].T
    blk = lax.broadcasted_iota(jnp.int32, (nbp, tq), 0)
    prow = lax.broadcasted_iota(jnp.int32, (LANES, tq), 0)
    klane = lax.broadcasted_iota(jnp.int32, (nbp, LANES), 1)
    key = lax.broadcasted_iota(jnp.int32, (MOBA_BLOCK, tq), 0)
    qry = lax.broadcasted_iota(jnp.int32, (MOBA_BLOCK, tq), 1)
    own0 = pl.multiple_of(i * MOBA_BLOCK, MOBA_BLOCK)

    def head_rows(h):
        return slice(h * HEAD_DIM, (h + 1) * HEAD_DIM)

    for p in range(npairs):
        qp = qt[p * LANES:(p + 1) * LANES]
        km = kmean_ref[:, p * LANES:(p + 1) * LANES]
        halves = []
        for w in range(2):
            h = 2 * p + w
            gate = jnp.dot(jnp.where(klane // HEAD_DIM == w, km, 0.0), qp, preferred_element_type=F32,
                           precision=lax.Precision.HIGHEST)
            sel = _top_mask_rows(jnp.where(blk < i, gate, NEG_INF), MOBA_TOPK)
            sel_sc[h] = jnp.where(sel, 1.0, 0.0)
            halves.append(jnp.where(prow // HEAD_DIM == w, qp * (LOG2_E / math.sqrt(HEAD_DIM)), 0.0).astype(BF16))
        wq = jnp.concatenate(halves, axis=1)
        wq_sc[p] = wq
        st = jnp.dot(k_ref[pl.ds(own0, MOBA_BLOCK), p * LANES:(p + 1) * LANES], wq, preferred_element_type=F32)
        for w in range(2):
            h = 2 * p + w
            s = jnp.where(key <= qry, st[:, w * tq:(w + 1) * tq], NEG_INF)
            m = jnp.max(s, axis=0, keepdims=True)
            pt = jnp.exp2(s - m)
            m_sc[h:h + 1, :] = m
            l_sc[h:h + 1, :] = jnp.sum(pt, axis=0, keepdims=True)
            acc_sc[head_rows(h), :] = jnp.dot(vt_ref[head_rows(h), pl.ds(own0, MOBA_BLOCK)], pt.astype(BF16),
                                              preferred_element_type=F32)

    def scores(n, slot):
        r0 = pl.multiple_of(n * MOBA_BLOCK, MOBA_BLOCK)
        for p in range(npairs):
            st_sc[slot, p] = jnp.dot(k_ref[pl.ds(r0, MOBA_BLOCK), p * LANES:(p + 1) * LANES], wq_sc[p],
                                     preferred_element_type=F32)

    def attend(n, slot):
        r0 = pl.multiple_of(n * MOBA_BLOCK, MOBA_BLOCK)
        for h in range(N_HEADS):
            s = st_sc[slot, h // 2, :, (h % 2) * tq:(h % 2 + 1) * tq]
            chosen = sel_sc[h, pl.ds(n, 1), :] > 0.0
            m_old = m_sc[h:h + 1, :]
            m_new = jnp.maximum(m_old, jnp.where(chosen, jnp.max(s, axis=0, keepdims=True), NEG_INF))
            alpha = jnp.exp2(m_old - m_new)
            pt = jnp.exp2(s - jnp.where(chosen, m_new, -NEG_INF))
            m_sc[h:h + 1, :] = m_new
            l_sc[h:h + 1, :] = alpha * l_sc[h:h + 1, :] + jnp.sum(pt, axis=0, keepdims=True)
            acc_sc[head_rows(h), :] = alpha * acc_sc[head_rows(h), :] + jnp.dot(
                vt_ref[head_rows(h), pl.ds(r0, MOBA_BLOCK)], pt.astype(BF16), preferred_element_type=F32)

    scores(0, 0)

    def body(k, _):
        n = 2 * k
        scores(n + 1, 1)
        attend(n, 0)
        scores(jnp.minimum(n + 2, i), 0)
        attend(n + 1, 1)
        return 0

    lax.fori_loop(0, (i + 1) // 2, body, 0)
    out_t = acc_sc[...].reshape(N_HEADS, HEAD_DIM, tq) / l_sc[...][:, None, :]
    o_ref[...] = out_t.reshape(W_MIX, tq).T


def _moba_prompt(qa, ka16, vat16, kmean):
    s = qa.shape[0]
    assert s % Q_TILE == 0 and Q_TILE == MOBA_BLOCK
    nb = kmean.shape[0]
    kmean = jnp.pad(kmean, ((0, -nb % 8), (0, 0)))
    return pl.pallas_call(
        _moba_prompt_kernel,
        grid=(s // Q_TILE,),
        in_specs=[pl.BlockSpec((Q_TILE, W_MIX), lambda i: (i, 0)), _resident(ka16.shape), _resident(vat16.shape),
                  _resident(kmean.shape)],
        out_specs=pl.BlockSpec((Q_TILE, W_MIX), lambda i: (i, 0)),
        out_shape=jax.ShapeDtypeStruct((s, W_MIX), F32),
        scratch_shapes=[pltpu.VMEM((N_HEADS // 2, LANES, 2 * Q_TILE), BF16),
                        pltpu.VMEM((N_HEADS, kmean.shape[0], Q_TILE), F32),
                        pltpu.VMEM((N_HEADS, Q_TILE), F32), pltpu.VMEM((N_HEADS, Q_TILE), F32),
                        pltpu.VMEM((W_MIX, Q_TILE), F32),
                        pltpu.VMEM((2, N_HEADS // 2, MOBA_BLOCK, 2 * Q_TILE), F32)],
        compiler_params=_cparams(("parallel",)),
        name="moba_prompt",
    )(qa, ka16, vat16, kmean)


def _sb_tile(z, mask, carry, tri):
    t = jnp.log1p(jnp.exp(-jnp.abs(z)))
    log_beta = jnp.minimum(z, 0.0) - t
    log_keep = jnp.where(mask, -jnp.maximum(z, 0.0) - t, 0.0)
    after = _split_dot(log_keep, tri) + carry
    a = jnp.where(mask, jnp.exp(log_beta + after), 0.0)
    return a, carry + jnp.sum(log_keep, axis=1, keepdims=True)


def _sb_prompt_kernel(q_ref, k_ref, v_ref, tri_ref, o_ref, acc_sc):
    i = pl.program_id(0)
    tq = Q_TILE
    lane = lax.broadcasted_iota(jnp.int32, (tq, LANES), 1)
    row = lax.broadcasted_iota(jnp.int32, (tq, tq), 0)
    col = lax.broadcasted_iota(jnp.int32, (tq, tq), 1)
    tri = tri_ref[...]
    nt = (((1,), (1,)), ((), ()))
    qes = []
    for h in range(N_HEADS):
        q2 = q_ref[:, (h // 2) * LANES:(h // 2 + 1) * LANES]
        qes.append(jnp.where(lane // HEAD_DIM == h % 2, q2 * (1.0 / math.sqrt(HEAD_DIM)), 0.0).astype(BF16))
    acc_sc[...] = jnp.zeros_like(acc_sc)

    def cond(state):
        j, carries = state
        worst = carries[0]
        for c in carries[1:]:
            worst = jnp.maximum(worst, c)
        return (j >= 0) & (jnp.max(worst) > SB_UNDERFLOW)

    def body(state):
        j, carries = state
        r0 = pl.multiple_of(j * tq, tq)
        mask = (col + (j - i) * tq) < row
        pair = lambda ref, h: ref[pl.ds(r0, tq), (h // 2) * LANES:(h // 2 + 1) * LANES]
        zs = [lax.dot_general(qes[h], pair(k_ref, h), nt, preferred_element_type=F32) for h in range(N_HEADS)]
        log_betas, afters, new_carries = [], [], []
        for h in range(N_HEADS):
            t = jnp.log1p(jnp.exp(-jnp.abs(zs[h])))
            log_betas.append(jnp.minimum(zs[h], 0.0) - t)
            log_keep = jnp.where(mask, -jnp.maximum(zs[h], 0.0) - t, 0.0)
            afters.append(_split_dot(log_keep, tri) + carries[h])
            new_carries.append(carries[h] + jnp.sum(log_keep, axis=1, keepdims=True))
        pvs = []
        for h in range(N_HEADS):
            a = jnp.where(mask, jnp.exp(log_betas[h] + afters[h]), 0.0)
            pvs.append(jnp.dot(a.astype(BF16), pair(v_ref, h), preferred_element_type=F32))
        for p in range(N_HEADS // 2):
            acc_sc[:, p * LANES:(p + 1) * LANES] += jnp.where(lane < HEAD_DIM, pvs[2 * p], pvs[2 * p + 1])
        return j - 1, tuple(new_carries)

    lax.while_loop(cond, body, (i, tuple(jnp.zeros((tq, 1), F32) for _ in range(N_HEADS))))
    o_ref[...] = acc_sc[...]


def _tri(n):
    r = jnp.arange(n)
    return (r[:, None] > r[None, :]).astype(BF16)


def _sb_prompt(qb, kb16, vb16):
    s = qb.shape[0]
    assert s % Q_TILE == 0
    return pl.pallas_call(
        _sb_prompt_kernel,
        grid=(s // Q_TILE,),
        in_specs=[pl.BlockSpec((Q_TILE, W_MIX), lambda i: (i, 0)), _resident(kb16.shape), _resident(vb16.shape),
                  _resident((Q_TILE, Q_TILE))],
        out_specs=pl.BlockSpec((Q_TILE, W_MIX), lambda i: (i, 0)),
        out_shape=jax.ShapeDtypeStruct((s, W_MIX), F32),
        scratch_shapes=[pltpu.VMEM((Q_TILE, W_MIX), F32)],
        compiler_params=_cparams(("parallel",)),
        name="sb_prompt",
    )(qb, kb16, vb16, _tri(Q_TILE))


PAGES_PER_STEP = 8
ROWS = LANES


def _head_block_diag(q):
    t = q.shape[0]
    lanehead = lax.broadcasted_iota(jnp.int32, q.shape, 1) // HEAD_DIM
    parts = [jnp.where(lanehead == h, q, 0.0) for h in range(N_HEADS)]
    parts.append(jnp.zeros((ROWS - N_HEADS * t, q.shape[1]), F32))
    return jnp.concatenate(parts, axis=0)


def _head_diag_extract(full, t):
    lanehead = lax.broadcasted_iota(jnp.int32, (t, full.shape[1]), 1) // HEAD_DIM
    out = jnp.zeros((t, full.shape[1]), F32)
    for h in range(N_HEADS):
        out = out + jnp.where(lanehead == h, full[h * t:(h + 1) * t], 0.0)
    return out


def _pad_rows(x):
    return jnp.concatenate([x, jnp.zeros((ROWS - x.shape[0], x.shape[1]), x.dtype)], axis=0)


def _moba_sample_kernel(pt_ref, q_ref, knew_ref, vnew_ref, *rest, n_blocks, t_new):
    kp = rest[:PAGES_PER_STEP]
    vp = rest[PAGES_PER_STEP:2 * PAGES_PER_STEP]
    o_ref, qbd_sc, m_sc, l_sc, acc_sc, kmean_sc = rest[2 * PAGES_PER_STEP:]
    g = pl.program_id(1)
    lane = lax.broadcasted_iota(jnp.int32, (ROWS, LANES), 1)
    row = lax.broadcasted_iota(jnp.int32, (ROWS, LANES), 0)

    @pl.when(g == 0)
    def _():
        qbd_sc[...] = _head_block_diag(q_ref[0])
        m_sc[...] = jnp.zeros_like(m_sc)
        l_sc[...] = jnp.zeros_like(l_sc)
        kmean_sc[...] = jnp.zeros_like(kmean_sc)

    qbd = qbd_sc[...]
    qs = (qbd * (1.0 / math.sqrt(HEAD_DIM))).astype(BF16)
    pages_per_block = MOBA_BLOCK // PAGE_SIZE
    blocks_per_step = PAGES_PER_STEP // pages_per_block
    klane = lax.broadcasted_iota(jnp.int32, kmean_sc.shape, 1)
    m_new, l_new, km_new = m_sc[...], l_sc[...], kmean_sc[...]
    kts = [jnp.concatenate([kp[jj * pages_per_block + r][...] for r in range(pages_per_block)], axis=1)
           for jj in range(blocks_per_step)]
    scores = [jnp.dot(qs, kt.astype(BF16), preferred_element_type=F32) for kt in kts]
    for jj in range(blocks_per_step):
        n = g * blocks_per_step + jj
        kt, s = kts[jj], scores[jj]
        vt = jnp.concatenate([vp[jj * pages_per_block + r][...] for r in range(pages_per_block)], axis=1)
        m = jnp.max(s, axis=1, keepdims=True)
        p = jnp.exp(s - m)
        m_new = jnp.where(lane == n, m, m_new)
        l_new = jnp.where(lane == n, jnp.sum(p, axis=1, keepdims=True), l_new)
        acc_sc[n] = lax.dot_general(p.astype(BF16), vt.astype(BF16), (((1,), (1,)), ((), ())),
                                    preferred_element_type=F32)
        km_new = jnp.where(klane == n, jnp.mean(kt, axis=1, keepdims=True), km_new)
    m_sc[...] = m_new
    l_sc[...] = l_new
    kmean_sc[...] = km_new

    @pl.when(g == pl.num_programs(1) - 1)
    def _():
        gate = jnp.dot(qbd, kmean_sc[...], preferred_element_type=F32, precision=lax.Precision.HIGHEST)
        sel = _top_mask_lanes(jnp.where(lane < n_blocks, gate, NEG_INF), MOBA_TOPK)
        m_all = m_sc[...]
        s_own = lax.dot_general(qs, _pad_rows(knew_ref[0]).astype(BF16), (((1,), (1,)), ((), ())),
                                preferred_element_type=F32)
        s_own = jnp.where((lane <= row % t_new) & (lane < t_new), s_own, NEG_INF)
        m_own = jnp.max(s_own, axis=1, keepdims=True)
        p_own = jnp.exp(s_own - m_own)
        m_tot = jnp.maximum(jnp.max(jnp.where(sel, m_all, NEG_INF), axis=1, keepdims=True), m_own)
        wgt = jnp.where(sel, jnp.exp(m_all - m_tot), 0.0)
        w_own = jnp.exp(m_own - m_tot)
        l_tot = jnp.sum(wgt * l_sc[...], axis=1, keepdims=True) + w_own * jnp.sum(p_own, axis=1, keepdims=True)
        out = w_own * jnp.dot(p_own.astype(BF16), _pad_rows(vnew_ref[0]).astype(BF16), preferred_element_type=F32)
        for nb in range(n_blocks):
            out = out + wgt[:, nb:nb + 1] * acc_sc[nb]
        o_ref[0] = _head_diag_extract(out / l_tot, t_new)


def _page_specs(layer):
    def spec(r):
        return pl.BlockSpec((None, None, W_MIX, PAGE_SIZE),
                            lambda b, g, pt: (layer, pt[b, g * PAGES_PER_STEP + r], 0, 0))
    return [spec(r) for r in range(PAGES_PER_STEP)]


def _moba_sample(page_table, qa, knew, vnew, cache_k, cache_v, layer):
    nseq, t_new, _ = qa.shape
    n_pages = page_table.shape[1]
    past = n_pages * PAGE_SIZE
    assert past % MOBA_BLOCK == 0 and t_new <= MOBA_BLOCK and t_new % 8 == 0 and N_HEADS * t_new <= ROWS
    assert n_pages % PAGES_PER_STEP == 0 and past // MOBA_BLOCK <= LANES
    n_blocks = past // MOBA_BLOCK
    new_spec = pl.BlockSpec((1, t_new, W_MIX), lambda b, g, pt: (b, 0, 0))
    grid_spec = pltpu.PrefetchScalarGridSpec(
        num_scalar_prefetch=1,
        grid=(nseq, n_pages // PAGES_PER_STEP),
        in_specs=[new_spec] * 3 + _page_specs(layer) + _page_specs(layer),
        out_specs=new_spec,
        scratch_shapes=[pltpu.VMEM((ROWS, W_MIX), F32), pltpu.VMEM((ROWS, LANES), F32),
                        pltpu.VMEM((ROWS, LANES), F32), pltpu.VMEM((n_blocks, ROWS, W_MIX), F32),
                        pltpu.VMEM((W_MIX, LANES), F32)],
    )
    return pl.pallas_call(
        functools.partial(_moba_sample_kernel, n_blocks=n_blocks, t_new=t_new),
        grid_spec=grid_spec,
        out_shape=jax.ShapeDtypeStruct((nseq, t_new, W_MIX), F32),
        compiler_params=_cparams(("parallel", "arbitrary")),
        name="moba_sample",
    )(page_table, qa, knew, vnew, *([cache_k] * PAGES_PER_STEP), *([cache_v] * PAGES_PER_STEP))


SB_EAGER_PAGES = 2


def _sb_sample_kernel(pt_ref, q_ref, knew_ref, vnew_ref, k1_ref, k2_ref, v1_ref, v2_ref, kc_ref, vc_ref, tri_ref,
                      o_ref, kbuf, vbuf, sem, *, layer, n_pages, t_new):
    b = pl.program_id(0)
    lane = lax.broadcasted_iota(jnp.int32, (ROWS, LANES), 1)
    row = lax.broadcasted_iota(jnp.int32, (ROWS, LANES), 0)
    real_row = lax.broadcasted_iota(jnp.int32, (ROWS, 1), 0) < N_HEADS * t_new
    tri = tri_ref[...]
    qs = (_head_block_diag(q_ref[0]) * (1.0 / math.sqrt(HEAD_DIM))).astype(BF16)

    nt = (((1,), (1,)), ((), ()))

    def tile(kt, vt, carry, acc):
        z = jnp.dot(qs, kt.astype(BF16), preferred_element_type=F32)
        a, carry = _sb_tile(z, every, carry, tri)
        return carry, acc + lax.dot_general(a.astype(BF16), vt.astype(BF16), nt, preferred_element_type=F32)

    every = lane >= 0
    new_mask = (lane < row % t_new) & (lane < t_new)
    z = lax.dot_general(qs, _pad_rows(knew_ref[0]).astype(BF16), nt, preferred_element_type=F32)
    a, carry = _sb_tile(z, new_mask, jnp.zeros((ROWS, 1), F32), tri)
    acc = jnp.dot(a.astype(BF16), _pad_rows(vnew_ref[0]).astype(BF16), preferred_element_type=F32)
    carry, acc = tile(k1_ref[...], v1_ref[...], carry, acc)
    carry, acc = tile(k2_ref[...], v2_ref[...], carry, acc)

    def cond(state):
        p, carry, _ = state
        return (p >= 0) & (jnp.max(jnp.where(real_row, carry, NEG_INF)) > SB_UNDERFLOW)

    def body(state):
        p, carry, acc = state
        page = pt_ref[b, p]
        ck = pltpu.make_async_copy(kc_ref.at[layer, page], kbuf, sem.at[0])
        cv = pltpu.make_async_copy(vc_ref.at[layer, page], vbuf, sem.at[1])
        ck.start()
        cv.start()
        ck.wait()
        cv.wait()
        carry, acc = tile(kbuf[...], vbuf[...], carry, acc)
        return p - 1, carry, acc

    _, _, acc = lax.while_loop(cond, body, (jnp.int32(n_pages - SB_EAGER_PAGES - 1), carry, acc))
    o_ref[0] = _head_diag_extract(acc, t_new)


def _sb_sample(page_table, qb, knew, vnew, cache_k, cache_v, layer):
    nseq, t_new, _ = qb.shape
    n_pages = page_table.shape[1]
    assert n_pages >= SB_EAGER_PAGES and t_new % 8 == 0 and N_HEADS * t_new <= ROWS and t_new <= LANES
    new_spec = pl.BlockSpec((1, t_new, W_MIX), lambda b, pt: (b, 0, 0))
    page = lambda back: pl.BlockSpec((None, None, W_MIX, PAGE_SIZE),
                                     lambda b, pt: (layer, pt[b, n_pages - back], 0, 0))
    grid_spec = pltpu.PrefetchScalarGridSpec(
        num_scalar_prefetch=1,
        grid=(nseq,),
        in_specs=[new_spec] * 3 + [page(1), page(2), page(1), page(2),
                                   pl.BlockSpec(memory_space=pl.ANY), pl.BlockSpec(memory_space=pl.ANY),
                                   pl.BlockSpec((PAGE_SIZE, PAGE_SIZE), lambda b, pt: (0, 0))],
        out_specs=new_spec,
        scratch_shapes=[pltpu.VMEM((W_MIX, PAGE_SIZE), F32), pltpu.VMEM((W_MIX, PAGE_SIZE), F32),
                        pltpu.SemaphoreType.DMA((2,))],
    )
    return pl.pallas_call(
        functools.partial(_sb_sample_kernel, layer=layer, n_pages=n_pages, t_new=t_new),
        grid_spec=grid_spec,
        out_shape=jax.ShapeDtypeStruct((nseq, t_new, W_MIX), F32),
        compiler_params=_cparams(("parallel",)),
        name="sb_sample",
    )(page_table, qb, knew, vnew, cache_k, cache_k, cache_v, cache_v, cache_k, cache_v, _tri(PAGE_SIZE))


PEER_HALF = 128


def _postmix_kernel(x_ref, oa_ref, ob_ref, sga_ref, sgb_ref, wua_ref, wub_ref, wo_ref, gffn_ref, wpqt_ref, sk_ref,
                    h_ref, hnt_ref, s1t_ref, s2t_ref):
    ya = jnp.dot(oa_ref[...].astype(BF16), wua_ref[...], preferred_element_type=F32)
    yb = jnp.dot(ob_ref[...].astype(BF16), wub_ref[...], preferred_element_type=F32)
    merged = sga_ref[...] * ya + sgb_ref[...] * yb
    h = x_ref[...] + jnp.dot(merged.astype(BF16), wo_ref[...], preferred_element_type=F32)
    h_ref[...] = h
    ms = jnp.mean(h * h, axis=-1, keepdims=True)
    hn = (h * lax.rsqrt(ms + RMS_EPS)) * gffn_ref[...]
    hnt = hn.T.astype(BF16)
    hnt_ref[...] = hnt
    pqt = jnp.dot(wpqt_ref[...], hnt, preferred_element_type=F32)
    for hh in range(PEER_HEADS):
        r0 = hh * 2 * PEER_HALF
        s1t_ref[hh] = jnp.dot(sk_ref[hh, 0], pqt[r0:r0 + PEER_HALF].astype(BF16), preferred_element_type=F32)
        s2t_ref[hh] = jnp.dot(sk_ref[hh, 1], pqt[r0 + PEER_HALF:r0 + 2 * PEER_HALF].astype(BF16),
                              preferred_element_type=F32)


def _postmix(x, oa, ob, sga, sgb, wua16, wub16, wo16, gffn, wpqt16, sk16):
    t = x.shape[0]
    assert t % TOK_TILE == 0
    tile = lambda wd: pl.BlockSpec((TOK_TILE, wd), lambda i: (i, 0))
    score = pl.BlockSpec((PEER_HEADS, PEER_NKEYS, TOK_TILE), lambda i: (0, 0, i))
    return pl.pallas_call(
        _postmix_kernel,
        grid=(t // TOK_TILE,),
        in_specs=[tile(D_MODEL), tile(W_MIX), tile(W_MIX), tile(D_MODEL), tile(D_MODEL),
                  _resident(wua16.shape), _resident(wub16.shape), _resident(wo16.shape),
                  pl.BlockSpec((1, D_MODEL), lambda i: (0, 0)), _resident(wpqt16.shape), _resident(sk16.shape)],
        out_specs=[tile(D_MODEL), pl.BlockSpec((D_MODEL, TOK_TILE), lambda i: (0, i)), score, score],
        out_shape=[jax.ShapeDtypeStruct((t, D_MODEL), F32), jax.ShapeDtypeStruct((D_MODEL, t), BF16),
                   jax.ShapeDtypeStruct((PEER_HEADS, PEER_NKEYS, t), F32),
                   jax.ShapeDtypeStruct((PEER_HEADS, PEER_NKEYS, t), F32)],
        compiler_params=_cparams(("parallel",)),
        name="postmix",
    )(x, oa, ob, sga, sgb, wua16, wub16, wo16, gffn.reshape(1, -1), wpqt16, sk16)


ROUTE_LANES = 512
ROUTE_CHUNK = 128
NOT_RANKED = 127.0


def _top_rows_pair(sa, sb, k):
    n, l = sa.shape
    riota = lax.broadcasted_iota(jnp.int32, (n, l), 0)
    kiota = lax.broadcasted_iota(jnp.int32, (k, l), 0)

    def step(r, s, rank, vals):
        m = jnp.max(s, axis=0, keepdims=True)
        idx = jnp.min(jnp.where(s == m, riota, n), axis=0, keepdims=True)
        pick = riota == idx
        return jnp.where(pick, -jnp.inf, s), jnp.where(pick, r, rank), jnp.where(kiota == r, m, vals)

    def body(r, carry):
        return step(r, *carry[:3]) + step(r, *carry[3:])

    init = (jnp.full((n, l), k, jnp.int32), jnp.zeros((k, l), F32))
    out = lax.fori_loop(0, k, body, (sa,) + init + (sb,) + init)
    return out[1], out[2], out[4], out[5]


def _pair_candidates(v1, v2, k):
    assert k == 16
    l = v1.shape[1]
    parts = [v1[0:1] + v2, v1[1:2] + v2[0:8]] + [v1[i:i + 1] + v2[0:8] for i in range(2, 8)] + [v1[8:16] + v2[0:1]]
    cand = jnp.concatenate(parts, axis=0)
    r = lax.broadcasted_iota(jnp.int32, cand.shape, 0)
    grp = r // 8
    last = cand.shape[0] // 8 - 1
    flat = jnp.where(grp <= 2, r, jnp.where(grp == last, (r - 8 * (last - 1)) * k, r + 8 * grp - k))
    return cand, flat


def _route_kernel(s1_ref, s2_ref, na_ref, wa_ref, rb_ref, e2_ref):
    k = PEER_TOPK

    def chunk(c, _):
        sl = pl.ds(pl.multiple_of(c * ROUTE_CHUNK, ROUTE_CHUNK), ROUTE_CHUNK)
        s1 = s1_ref[0, :, sl]
        s2 = s2_ref[0, :, sl]
        rank1, v1, rank2, v2 = _top_rows_pair(s1, s2, k)
        cand, flat = _pair_candidates(v1, v2, k)
        l = cand.shape[1]
        kiota = lax.broadcasted_iota(jnp.int32, (k, l), 0)
        top = cand[0:1]

        def body(r, carry):
            cnd, cnt, z = carry
            m = jnp.max(cnd, axis=0, keepdims=True)
            idx = jnp.min(jnp.where(cnd == m, flat, k * k), axis=0, keepdims=True)
            cnt = jnp.where(kiota == idx // k, cnt + 1.0, cnt)
            return jnp.where(flat == idx, -jnp.inf, cnd), cnt, z + jnp.exp(m - top)

        _, cnt, z = lax.fori_loop(0, k, body, (cand, jnp.zeros((k, l), F32), jnp.zeros((1, l), F32)))
        na = jnp.zeros(s1.shape, F32)
        for i in range(k):
            na = jnp.where(rank1 == i, cnt[i:i + 1], na)
        na_ref[0, :, sl] = na
        wa_ref[0, :, sl] = jnp.where(rank1 < k, jnp.exp(s1 - v1[0:1]) / z, 0.0)
        rb_ref[0, :, sl] = jnp.where(rank2 < k, rank2.astype(F32), NOT_RANKED).astype(BF16)
        e2_ref[0, :, sl] = jnp.where(rank2 < k, jnp.exp(s2 - v2[0:1]), 0.0).astype(BF16)
        return 0

    lax.fori_loop(0, ROUTE_LANES // ROUTE_CHUNK, chunk, 0)


def _route(s1t, s2t):
    t = s1t.shape[2]
    assert t % ROUTE_LANES == 0
    blk = pl.BlockSpec((1, PEER_NKEYS, ROUTE_LANES), lambda i, h: (h, 0, i))
    f32 = jax.ShapeDtypeStruct(s1t.shape, F32)
    b16 = jax.ShapeDtypeStruct(s1t.shape, BF16)
    return pl.pallas_call(
        _route_kernel,
        grid=(t // ROUTE_LANES, PEER_HEADS),
        in_specs=[blk, blk],
        out_specs=[blk] * 4,
        out_shape=[f32, f32, b16, b16],
        compiler_params=_cparams(("parallel", "parallel")),
        name="peer_route",
    )(s1t, s2t)


PEER_TOK = 512
A_PER_STEP = 8
BF16_SUBLANES = 16


def _peer_kernel(hnt_ref, na_ref, wa_ref, rb_ref, e2_ref, u_ref, v_ref, o_ref, acc_sc):
    j = pl.program_id(1)

    @pl.when(j == 0)
    def _():
        acc_sc[...] = jnp.zeros_like(acc_sc)

    hnt = hnt_ref[...]
    tok = hnt.shape[1]

    def row16(ref, h, aa):
        slab = jnp.broadcast_to(ref[h, aa:aa + 1, :], (BF16_SUBLANES, tok)).astype(BF16)
        return jnp.concatenate([slab] * (PEER_NKEYS // BF16_SUBLANES), axis=0)

    parts = []
    for aa in range(A_PER_STEP):
        act = jnp.dot(u_ref[aa * PEER_NKEYS:(aa + 1) * PEER_NKEYS, :], hnt, preferred_element_type=F32)
        gel = 0.5 * act * (1.0 + lax.erf(act * (1.0 / math.sqrt(2.0))))
        gate = jnp.zeros(act.shape, BF16)
        for h in range(PEER_HEADS):
            gate = gate + jnp.where(rb_ref[h] < row16(na_ref, h, aa), e2_ref[h], 0.0) * row16(wa_ref, h, aa)
        parts.append(gate * gel.astype(BF16))
    pt = jnp.concatenate(parts, axis=0)
    acc_sc[...] += lax.dot_general(pt, v_ref[...], (((0,), (0,)), ((), ())), preferred_element_type=F32)

    @pl.when(j == pl.num_programs(1) - 1)
    def _():
        o_ref[...] = acc_sc[...]


def _peer(hnt, na, wa, rb, e2, u16, v16):
    t = hnt.shape[1]
    assert t % PEER_TOK == 0 and u16.shape[0] == PEER_NKEYS * PEER_NKEYS
    n_exp = A_PER_STEP * PEER_NKEYS
    route = pl.BlockSpec((PEER_HEADS, PEER_NKEYS, PEER_TOK), lambda i, j: (0, 0, i))
    a_rows = pl.BlockSpec((PEER_HEADS, A_PER_STEP, PEER_TOK), lambda i, j: (0, j, i))
    table = pl.BlockSpec((n_exp, D_MODEL), lambda i, j: (j, 0))
    return pl.pallas_call(
        _peer_kernel,
        grid=(t // PEER_TOK, PEER_NKEYS // A_PER_STEP),
        in_specs=[pl.BlockSpec((D_MODEL, PEER_TOK), lambda i, j: (0, i)), a_rows, a_rows, route, route, table, table],
        out_specs=pl.BlockSpec((PEER_TOK, D_MODEL), lambda i, j: (i, 0)),
        out_shape=jax.ShapeDtypeStruct((t, D_MODEL), F32),
        scratch_shapes=[pltpu.VMEM((PEER_TOK, D_MODEL), F32)],
        compiler_params=_cparams(("parallel", "arbitrary")),
        name="peer_experts",
    )(hnt, na, wa, rb, e2, u16, v16)


def _ple_kernel(h_ref, f_ref, p_ref, gple_ref, wg_ref, wp_ref, y_ref):
    h = h_ref[...] + f_ref[...]
    ms = jnp.mean(h * h, axis=-1, keepdims=True)
    hn = ((h * lax.rsqrt(ms + RMS_EPS)) * gple_ref[...]).astype(BF16)
    gate = jax.nn.sigmoid(jnp.dot(hn, wg_ref[...], preferred_element_type=F32))
    y_ref[...] = h + gate * jnp.dot(p_ref[...].astype(BF16), wp_ref[...], preferred_element_type=F32)


def _ple(h, ffn, p, gple, wg16, wp16):
    t = h.shape[0]
    assert t % TOK_TILE == 0
    tile = lambda wd: pl.BlockSpec((TOK_TILE, wd), lambda i: (i, 0))
    return pl.pallas_call(
        _ple_kernel,
        grid=(t // TOK_TILE,),
        in_specs=[tile(D_MODEL), tile(D_MODEL), tile(p.shape[1]), pl.BlockSpec((1, D_MODEL), lambda i: (0, 0)),
                  _resident(wg16.shape), _resident(wp16.shape)],
        out_specs=tile(D_MODEL),
        out_shape=jax.ShapeDtypeStruct((t, D_MODEL), F32),
        compiler_params=_cparams(("parallel",)),
        name="ple_gate",
    )(h, ffn, p, gple.reshape(1, -1), wg16, wp16)


def _post_mix(x, oa, ob, sga, sgb, p, wts):
    h, hnt, s1t, s2t = _postmix(x, oa, ob, sga, sgb, wts["wua"], wts["wub"], wts["wo"], wts["gffn"],
                                wts["wpqt"], wts["sk"])
    na, wa, rb, e2 = _route(s1t, s2t)
    ffn = _peer(hnt, na, wa, rb, e2, wts["u"], wts["v"])
    return _ple(h, ffn, p, wts["gple"], wts["wg"], wts["wp"])


def kernel(x_prompt, x_sample, cache_k_moba, cache_v_moba, cache_k_sb, cache_v_sb, page_table, p_prompt, p_sample,
           norm_mix, w_in, q_norm_moba, k_norm_moba, w_up_moba, w_up_sb, w_o, norm_ffn, w_peer_q, peer_sub_keys,
           peer_u, peer_v, norm_ple, w_ple_gate, w_ple):
    depth = w_in.shape[0]
    nb, s, _ = x_prompt.shape
    nseq, t_new, _ = x_sample.shape
    past = page_table.shape[1] * PAGE_SIZE
    pos_p = jnp.arange(s, dtype=jnp.int32)
    pos_s = jnp.tile(past + jnp.arange(t_new, dtype=jnp.int32), nseq)
    paged = lambda c: jnp.transpose(c, (0, 1, 3, 4, 2)).reshape(c.shape[0], c.shape[1], W_MIX, PAGE_SIZE)
    ckm, cvm, cks, cvs = paged(cache_k_moba), paged(cache_v_moba), paged(cache_k_sb), paged(cache_v_sb)
    heads = lambda a, lead: a.reshape(*lead, N_HEADS, HEAD_DIM)
    heads_t = lambda a: jnp.transpose(a.reshape(N_HEADS, HEAD_DIM, -1), (2, 0, 1))

    hp = [x_prompt[b] for b in range(nb)]
    hs = x_sample.reshape(nseq * t_new, D_MODEL)
    kv_p = [[] for _ in range(4)]
    kv_s = [[] for _ in range(4)]
    for l in range(depth):
        w_in16 = w_in[l].astype(BF16)
        wts = dict(wua=w_up_moba[l].astype(BF16), wub=w_up_sb[l].astype(BF16), wo=w_o[l].astype(BF16),
                   gffn=norm_ffn[l], wpqt=w_peer_q[l].T.astype(BF16), sk=peer_sub_keys[l].astype(BF16),
                   u=peer_u[l].astype(BF16), v=peer_v[l].astype(BF16), gple=norm_ple[l],
                   wg=w_ple_gate[l].astype(BF16), wp=w_ple[l].astype(BF16))
        layer_kv = [[] for _ in range(4)]
        for b in range(nb):
            (qa, ka, va, qb, kb, vb, sga, sgb, ka16, va16, kb16, vb16, kmean) = _inproj(
                hp[b], pos_p, norm_mix[l], w_in16, q_norm_moba[l], k_norm_moba[l], True)
            oa = _moba_prompt(qa, ka16, va16, kmean.reshape(-1, W_MIX))
            ob = _sb_prompt(qb, kb16, vb16)
            hp[b] = _post_mix(hp[b], oa, ob, sga, sgb, p_prompt[l, b], wts)
            for dst, a in zip(layer_kv, (ka, va, kb, vb)):
                dst.append(heads_t(a))
        for dst, src in zip(kv_p, layer_kv):
            dst.append(jnp.stack(src))

        (qa, ka, va, qb, kb, vb, sga, sgb, *_) = _inproj(hs, pos_s, norm_mix[l], w_in16, q_norm_moba[l],
                                                          k_norm_moba[l], False)
        seq = lambda a: a.reshape(nseq, t_new, W_MIX)
        oa = _moba_sample(page_table, seq(qa), seq(ka), seq(va), ckm, cvm, l)
        ob = _sb_sample(page_table, seq(qb), seq(kb), seq(vb), cks, cvs, l)
        hs = _post_mix(hs, oa.reshape(-1, W_MIX), ob.reshape(-1, W_MIX), sga, sgb,
                       p_sample[l].reshape(-1, p_sample.shape[-1]), wts)
        for dst, a in zip(kv_s, (ka, va, kb, vb)):
            dst.append(heads(a, (nseq, t_new)))

    return (jnp.stack(hp), hs.reshape(nseq, t_new, D_MODEL),
            *[jnp.stack(a) for a in kv_p], *[jnp.stack(a) for a in kv_s])
```

```python
import functools
import math

import jax
import jax.numpy as jnp
from jax import lax
from jax.experimental import pallas as pl
from jax.experimental.pallas import tpu as pltpu

F32 = jnp.float32
BF16 = jnp.bfloat16

D_MODEL = 1024
HEAD_DIM = 64
N_HEADS = 8
W_MIX = N_HEADS * HEAD_DIM
PAGE_SIZE = 128
MOBA_BLOCK = 256
MOBA_TOPK = 3
ROPE_THETA = 10000.0
PEER_HEADS = 8
PEER_NKEYS = 128
PEER_TOPK = 16
RMS_EPS = 1e-6
NEG_INF = -1e30
LANES = 128
SB_UNDERFLOW = -104.0
VMEM_LIMIT = 56 * 1024 * 1024

TOK_TILE = 512
Q_TILE = 256


def _cparams(sem):
    return pltpu.CompilerParams(dimension_semantics=sem, vmem_limit_bytes=VMEM_LIMIT)


def _resident(shape):
    nd = len(shape)
    return pl.BlockSpec(shape, lambda *_: (0,) * nd, pipeline_mode=pl.Buffered(1))


def _split_dot(x, m):
    hi = x.astype(BF16)
    lo = (x - hi.astype(F32)).astype(BF16)
    return (jnp.dot(hi, m, preferred_element_type=F32) + jnp.dot(lo, m, preferred_element_type=F32))


def _inproj_kernel(x_ref, gmix_ref, w_ref, qg_ref, kg_ref, cos_ref, sin_ref, bavg_ref,
                   qa_ref, ka_ref, va_ref, qb_ref, kb_ref, vb_ref, sga_ref, sgb_ref,
                   ka16_ref, va16_ref, kb16_ref, vb16_ref, kmean_ref, *, kv_tokens_on_lanes):
    kv_out = (lambda a: a.T) if kv_tokens_on_lanes else (lambda a: a)
    x = x_ref[...]
    ms = jnp.mean(x * x, axis=-1, keepdims=True)
    xn = ((x * lax.rsqrt(ms + RMS_EPS)) * gmix_ref[...]).astype(BF16)

    def proj(c0, c1):
        return jnp.dot(xn, w_ref[:, c0:c1], preferred_element_type=F32)

    cos = jnp.concatenate([cos_ref[...]] * (W_MIX // LANES), axis=1)
    sin = jnp.concatenate([sin_ref[...]] * (W_MIX // LANES), axis=1)
    lane = lax.broadcasted_iota(jnp.int32, (x.shape[0], W_MIX), 1)
    low_half = (lane % HEAD_DIM) < (HEAD_DIM // 2)
    bavg = bavg_ref[...]

    def headnorm_rope(z, gain):
        msq = _split_dot(z * z, bavg)
        y = (z * lax.rsqrt(msq + RMS_EPS)) * gain
        partner = jnp.where(low_half, pltpu.roll(y, W_MIX - HEAD_DIM // 2, 1),
                            pltpu.roll(y, HEAD_DIM // 2, 1))
        return y * cos + partner * sin

    w = W_MIX
    qa_ref[...] = headnorm_rope(proj(0, w), qg_ref[...])
    ka = headnorm_rope(proj(w, 2 * w), kg_ref[...])
    ka_ref[...] = kv_out(ka)
    ka16_ref[...] = ka.astype(BF16)
    nblk = ka.shape[0] // MOBA_BLOCK
    kmean_ref[0] = jnp.mean(ka.reshape(nblk, MOBA_BLOCK, W_MIX), axis=1)
    va = proj(2 * w, 3 * w)
    va_ref[...] = kv_out(va)
    va16_ref[...] = kv_out(va).astype(BF16)
    qb_ref[...] = proj(3 * w, 4 * w)
    kb = proj(4 * w, 5 * w)
    kb_ref[...] = kv_out(kb)
    kb16_ref[...] = kb.astype(BF16)
    vb = proj(5 * w, 6 * w)
    vb_ref[...] = kv_out(vb)
    vb16_ref[...] = vb.astype(BF16)
    sga_ref[...] = jax.nn.sigmoid(proj(6 * w, 6 * w + D_MODEL))
    sgb_ref[...] = jax.nn.sigmoid(proj(6 * w + D_MODEL, 6 * w + 2 * D_MODEL))


def _rope_tables(pos):
    half = HEAD_DIM // 2
    inv = ROPE_THETA ** (-jnp.arange(half, dtype=F32) / half)
    ang = pos.astype(F32)[:, None] * inv[None, :]
    cos, sin = jnp.cos(ang), jnp.sin(ang)
    reps = LANES // HEAD_DIM
    return (jnp.concatenate([cos, cos] * reps, axis=1), jnp.concatenate([-sin, sin] * reps, axis=1))


def _inproj(x, pos, gmix, w_in16, qgain, kgain, kv_tokens_on_lanes):
    t = x.shape[0]
    assert t % TOK_TILE == 0
    nt = t // TOK_TILE
    cos, sin = _rope_tables(pos)
    head = jnp.arange(W_MIX) // HEAD_DIM
    bavg = jnp.where(head[:, None] == head[None, :], 1.0 / HEAD_DIM, 0.0).astype(BF16)
    tile = lambda wd: pl.BlockSpec((TOK_TILE, wd), lambda i: (i, 0))
    row = lambda wd: pl.BlockSpec((1, wd), lambda i: (0, 0))
    f32_w = jax.ShapeDtypeStruct((t, W_MIX), F32)
    b16_w = jax.ShapeDtypeStruct((t, W_MIX), BF16)
    f32_d = jax.ShapeDtypeStruct((t, D_MODEL), F32)
    nblk = TOK_TILE // MOBA_BLOCK
    if kv_tokens_on_lanes:
        kv_spec = pl.BlockSpec((W_MIX, TOK_TILE), lambda i: (0, i))
        kv_shape = jax.ShapeDtypeStruct((W_MIX, t), F32)
        va16_shape = jax.ShapeDtypeStruct((W_MIX, t), BF16)
    else:
        kv_spec, kv_shape, va16_shape = tile(W_MIX), f32_w, b16_w
    return pl.pallas_call(
        functools.partial(_inproj_kernel, kv_tokens_on_lanes=kv_tokens_on_lanes),
        grid=(nt,),
        in_specs=[tile(D_MODEL), row(D_MODEL), _resident(w_in16.shape), row(W_MIX), row(W_MIX),
                  tile(LANES), tile(LANES), _resident((W_MIX, W_MIX))],
        out_specs=[tile(W_MIX), kv_spec, kv_spec] * 2 + [tile(D_MODEL)] * 2
                  + [tile(W_MIX), kv_spec, tile(W_MIX), tile(W_MIX)]
                  + [pl.BlockSpec((1, nblk, W_MIX), lambda i: (i, 0, 0))],
        out_shape=[f32_w, kv_shape, kv_shape] * 2 + [f32_d] * 2 + [b16_w, va16_shape, b16_w, b16_w]
                  + [jax.ShapeDtypeStruct((nt, nblk, W_MIX), F32)],
        compiler_params=_cparams(("parallel",)),
        name="inproj",
    )(x, gmix.reshape(1, -1), w_in16, jnp.tile(qgain, N_HEADS).reshape(1, -1),
      jnp.tile(kgain, N_HEADS).reshape(1, -1), cos, sin, bavg)


def _top_mask_lanes(g, count):
    lane = lax.broadcasted_iota(jnp.int32, g.shape, 1)
    sel = jnp.zeros(g.shape, jnp.bool_)
    for _ in range(count):
        m = jnp.max(g, axis=1, keepdims=True)
        idx = jnp.min(jnp.where(g == m, lane, g.shape[1]), axis=1, keepdims=True)
        pick = (lane == idx) & (m > NEG_INF / 2)
        sel = sel | pick
        g = jnp.where(pick, NEG_INF, g)
    return sel


LOG2_E = math.log2(math.e)


def _top_mask_rows(g, count):
    riota = lax.broadcasted_iota(jnp.int32, g.shape, 0)
    sel = jnp.zeros(g.shape, jnp.bool_)
    for _ in range(count):
        m = jnp.max(g, axis=0, keepdims=True)
        idx = jnp.min(jnp.where(g == m, riota, g.shape[0]), axis=0, keepdims=True)
        pick = (riota == idx) & (m > NEG_INF / 2)
        sel = sel | pick
        g = jnp.where(pick, NEG_INF, g)
    return sel


def _moba_prompt_kernel(q_ref, k_ref, vt_ref, kmean_ref, o_ref, wq_sc, sel_sc, m_sc, l_sc, acc_sc, st_sc):
    i = pl.program_id(0)
    tq = Q_TILE
    npairs = N_HEADS // 2
    nbp = kmean_ref.shape[0]
    qt = q_ref[...].T
    blk = lax.broadcasted_iota(jnp.int32, (nbp, tq), 0)
    prow = lax.broadcasted_iota(jnp.int32, (LANES, tq), 0)
    klane = lax.broadcasted_iota(jnp.int32, (nbp, LANES), 1)
    key = lax.broadcasted_iota(jnp.int32, (MOBA_BLOCK, tq), 0)
    qry = lax.broadcasted_iota(jnp.int32, (MOBA_BLOCK, tq), 1)
    own0 = pl.multiple_of(i * MOBA_BLOCK, MOBA_BLOCK)

    def head_rows(h):
        return slice(h * HEAD_DIM, (h + 1) * HEAD_DIM)

    for p in range(npairs):
        qp = qt[p * LANES:(p + 1) * LANES]
        km = kmean_ref[:, p * LANES:(p + 1) * LANES]
        halves = []
        for w in range(2):
            h = 2 * p + w
            gate = jnp.dot(jnp.where(klane // HEAD_DIM == w, km, 0.0), qp, preferred_element_type=F32,
                           precision=lax.Precision.HIGHEST)
            sel = _top_mask_rows(jnp.where(blk < i, gate, NEG_INF), MOBA_TOPK)
            sel_sc[h] = jnp.where(sel, 1.0, 0.0)
            halves.append(jnp.where(prow // HEAD_DIM == w, qp * (LOG2_E / math.sqrt(HEAD_DIM)), 0.0).astype(BF16))
        wq = jnp.concatenate(halves, axis=1)
        wq_sc[p] = wq
        st = jnp.dot(k_ref[pl.ds(own0, MOBA_BLOCK), p * LANES:(p + 1) * LANES], wq, preferred_element_type=F32)
        for w in range(2):
            h = 2 * p + w
            s = jnp.where(key <= qry, st[:, w * tq:(w + 1) * tq], NEG_INF)
            m = jnp.max(s, axis=0, keepdims=True)
            pt = jnp.exp2(s - m)
            m_sc[h:h + 1, :] = m
            l_sc[h:h + 1, :] = jnp.sum(pt, axis=0, keepdims=True)
            acc_sc[head_rows(h), :] = jnp.dot(vt_ref[head_rows(h), pl.ds(own0, MOBA_BLOCK)], pt.astype(BF16),
                                              preferred_element_type=F32)

    def scores(n, slot):
        r0 = pl.multiple_of(n * MOBA_BLOCK, MOBA_BLOCK)
        for p in range(npairs):
            st_sc[slot, p] = jnp.dot(k_ref[pl.ds(r0, MOBA_BLOCK), p * LANES:(p + 1) * LANES], wq_sc[p],
                                     preferred_element_type=F32)

    def attend(n, slot):
        r0 = pl.multiple_of(n * MOBA_BLOCK, MOBA_BLOCK)
        for h in range(N_HEADS):
            s = st_sc[slot, h // 2, :, (h % 2) * tq:(h % 2 + 1) * tq]
            chosen = sel_sc[h, pl.ds(n, 1), :] > 0.0
            m_old = m_sc[h:h + 1, :]
            m_new = jnp.maximum(m_old, jnp.where(chosen, jnp.max(s, axis=0, keepdims=True), NEG_INF))
            alpha = jnp.exp2(m_old - m_new)
            pt = jnp.exp2(s - jnp.where(chosen, m_new, -NEG_INF))
            m_sc[h:h + 1, :] = m_new
            l_sc[h:h + 1, :] = alpha * l_sc[h:h + 1, :] + jnp.sum(pt, axis=0, keepdims=True)
            acc_sc[head_rows(h), :] = alpha * acc_sc[head_rows(h), :] + jnp.dot(
                vt_ref[head_rows(h), pl.ds(r0, MOBA_BLOCK)], pt.astype(BF16), preferred_element_type=F32)

    scores(0, 0)

    def body(k, _):
        n = 2 * k
        scores(n + 1, 1)
        attend(n, 0)
        scores(jnp.minimum(n + 2, i), 0)
        attend(n + 1, 1)
        return 0

    lax.fori_loop(0, (i + 1) // 2, body, 0)
    out_t = acc_sc[...].reshape(N_HEADS, HEAD_DIM, tq) / l_sc[...][:, None, :]
    o_ref[...] = out_t.reshape(W_MIX, tq).T


def _moba_prompt(qa, ka16, vat16, kmean):
    s = qa.shape[0]
    assert s % Q_TILE == 0 and Q_TILE == MOBA_BLOCK
    nb = kmean.shape[0]
    kmean = jnp.pad(kmean, ((0, -nb % 8), (0, 0)))
    return pl.pallas_call(
        _moba_prompt_kernel,
        grid=(s // Q_TILE,),
        in_specs=[pl.BlockSpec((Q_TILE, W_MIX), lambda i: (i, 0)), _resident(ka16.shape), _resident(vat16.shape),
                  _resident(kmean.shape)],
        out_specs=pl.BlockSpec((Q_TILE, W_MIX), lambda i: (i, 0)),
        out_shape=jax.ShapeDtypeStruct((s, W_MIX), F32),
        scratch_shapes=[pltpu.VMEM((N_HEADS // 2, LANES, 2 * Q_TILE), BF16),
                        pltpu.VMEM((N_HEADS, kmean.shape[0], Q_TILE), F32),
                        pltpu.VMEM((N_HEADS, Q_TILE), F32), pltpu.VMEM((N_HEADS, Q_TILE), F32),
                        pltpu.VMEM((W_MIX, Q_TILE), F32),
                        pltpu.VMEM((2, N_HEADS // 2, MOBA_BLOCK, 2 * Q_TILE), F32)],
        compiler_params=_cparams(("parallel",)),
        name="moba_prompt",
    )(qa, ka16, vat16, kmean)


def _sb_tile(z, mask, carry, tri):
    t = jnp.log1p(jnp.exp(-jnp.abs(z)))
    log_beta = jnp.minimum(z, 0.0) - t
    log_keep = jnp.where(mask, -jnp.maximum(z, 0.0) - t, 0.0)
    after = _split_dot(log_keep, tri) + carry
    a = jnp.where(mask, jnp.exp(log_beta + after), 0.0)
    return a, carry + jnp.sum(log_keep, axis=1, keepdims=True)


def _sb_prompt_kernel(q_ref, k_ref, v_ref, tri_ref, o_ref, acc_sc):
    i = pl.program_id(0)
    tq = Q_TILE
    lane = lax.broadcasted_iota(jnp.int32, (tq, LANES), 1)
    row = lax.broadcasted_iota(jnp.int32, (tq, tq), 0)
    col = lax.broadcasted_iota(jnp.int32, (tq, tq), 1)
    tri = tri_ref[...]
    nt = (((1,), (1,)), ((), ()))
    qes = []
    for h in range(N_HEADS):
        q2 = q_ref[:, (h // 2) * LANES:(h // 2 + 1) * LANES]
        qes.append(jnp.where(lane // HEAD_DIM == h % 2, q2 * (1.0 / math.sqrt(HEAD_DIM)), 0.0).astype(BF16))
    acc_sc[...] = jnp.zeros_like(acc_sc)

    def cond(state):
        j, carries = state
        worst = carries[0]
        for c in carries[1:]:
            worst = jnp.maximum(worst, c)
        return (j >= 0) & (jnp.max(worst) > SB_UNDERFLOW)

    def body(state):
        j, carries = state
        r0 = pl.multiple_of(j * tq, tq)
        mask = (col + (j - i) * tq) < row
        pair = lambda ref, h: ref[pl.ds(r0, tq), (h // 2) * LANES:(h // 2 + 1) * LANES]
        zs = [lax.dot_general(qes[h], pair(k_ref, h), nt, preferred_element_type=F32) for h in range(N_HEADS)]
        log_betas, afters, new_carries = [], [], []
        for h in range(N_HEADS):
            t = jnp.log1p(jnp.exp(-jnp.abs(zs[h])))
            log_betas.append(jnp.minimum(zs[h], 0.0) - t)
            log_keep = jnp.where(mask, -jnp.maximum(zs[h], 0.0) - t, 0.0)
            afters.append(_split_dot(log_keep, tri) + carries[h])
            new_carries.append(carries[h] + jnp.sum(log_keep, axis=1, keepdims=True))
        pvs = []
        for h in range(N_HEADS):
            a = jnp.where(mask, jnp.exp(log_betas[h] + afters[h]), 0.0)
            pvs.append(jnp.dot(a.astype(BF16), pair(v_ref, h), preferred_element_type=F32))
        for p in range(N_HEADS // 2):
            acc_sc[:, p * LANES:(p + 1) * LANES] += jnp.where(lane < HEAD_DIM, pvs[2 * p], pvs[2 * p + 1])
        return j - 1, tuple(new_carries)

    lax.while_loop(cond, body, (i, tuple(jnp.zeros((tq, 1), F32) for _ in range(N_HEADS))))
    o_ref[...] = acc_sc[...]


def _tri(n):
    r = jnp.arange(n)
    return (r[:, None] > r[None, :]).astype(BF16)


def _sb_prompt(qb, kb16, vb16):
    s = qb.shape[0]
    assert s % Q_TILE == 0
    return pl.pallas_call(
        _sb_prompt_kernel,
        grid=(s // Q_TILE,),
        in_specs=[pl.BlockSpec((Q_TILE, W_MIX), lambda i: (i, 0)), _resident(kb16.shape), _resident(vb16.shape),
                  _resident((Q_TILE, Q_TILE))],
        out_specs=pl.BlockSpec((Q_TILE, W_MIX), lambda i: (i, 0)),
        out_shape=jax.ShapeDtypeStruct((s, W_MIX), F32),
        scratch_shapes=[pltpu.VMEM((Q_TILE, W_MIX), F32)],
        compiler_params=_cparams(("parallel",)),
        name="sb_prompt",
    )(qb, kb16, vb16, _tri(Q_TILE))


PAGES_PER_STEP = 16
ROWS = LANES


def _head_block_diag(q):
    t = q.shape[0]
    lanehead = lax.broadcasted_iota(jnp.int32, q.shape, 1) // HEAD_DIM
    parts = [jnp.where(lanehead == h, q, 0.0) for h in range(N_HEADS)]
    parts.append(jnp.zeros((ROWS - N_HEADS * t, q.shape[1]), F32))
    return jnp.concatenate(parts, axis=0)


def _head_diag_extract(full, t):
    lanehead = lax.broadcasted_iota(jnp.int32, (t, full.shape[1]), 1) // HEAD_DIM
    out = jnp.zeros((t, full.shape[1]), F32)
    for h in range(N_HEADS):
        out = out + jnp.where(lanehead == h, full[h * t:(h + 1) * t], 0.0)
    return out


def _pad_rows(x):
    return jnp.concatenate([x, jnp.zeros((ROWS - x.shape[0], x.shape[1]), x.dtype)], axis=0)


def _moba_sample_kernel(pt_ref, q_ref, knew_ref, vnew_ref, *rest, n_blocks, t_new):
    kp = rest[:PAGES_PER_STEP]
    vp = rest[PAGES_PER_STEP:2 * PAGES_PER_STEP]
    o_ref, qbd_sc, m_sc, l_sc, acc_sc, kmean_sc = rest[2 * PAGES_PER_STEP:]
    g = pl.program_id(1)
    lane = lax.broadcasted_iota(jnp.int32, (ROWS, LANES), 1)
    row = lax.broadcasted_iota(jnp.int32, (ROWS, LANES), 0)

    @pl.when(g == 0)
    def _():
        qbd_sc[...] = _head_block_diag(q_ref[0])
        m_sc[...] = jnp.zeros_like(m_sc)
        l_sc[...] = jnp.zeros_like(l_sc)
        kmean_sc[...] = jnp.zeros_like(kmean_sc)

    qbd = qbd_sc[...]
    qs = (qbd * (1.0 / math.sqrt(HEAD_DIM))).astype(BF16)
    pages_per_block = MOBA_BLOCK // PAGE_SIZE
    blocks_per_step = PAGES_PER_STEP // pages_per_block
    klane = lax.broadcasted_iota(jnp.int32, kmean_sc.shape, 1)
    m_new, l_new, km_new = m_sc[...], l_sc[...], kmean_sc[...]
    kts = [jnp.concatenate([kp[jj * pages_per_block + r][...] for r in range(pages_per_block)], axis=1)
           for jj in range(blocks_per_step)]
    scores = [jnp.dot(qs, kt.astype(BF16), preferred_element_type=F32) for kt in kts]
    for jj in range(blocks_per_step):
        n = g * blocks_per_step + jj
        kt, s = kts[jj], scores[jj]
        vt = jnp.concatenate([vp[jj * pages_per_block + r][...] for r in range(pages_per_block)], axis=1)
        m = jnp.max(s, axis=1, keepdims=True)
        p = jnp.exp(s - m)
        m_new = jnp.where(lane == n, m, m_new)
        l_new = jnp.where(lane == n, jnp.sum(p, axis=1, keepdims=True), l_new)
        acc_sc[n] = lax.dot_general(p.astype(BF16), vt.astype(BF16), (((1,), (1,)), ((), ())),
                                    preferred_element_type=F32)
        km_new = jnp.where(klane == n, jnp.mean(kt, axis=1, keepdims=True), km_new)
    m_sc[...] = m_new
    l_sc[...] = l_new
    kmean_sc[...] = km_new

    @pl.when(g == pl.num_programs(1) - 1)
    def _():
        gate = jnp.dot(qbd, kmean_sc[...], preferred_element_type=F32, precision=lax.Precision.HIGHEST)
        sel = _top_mask_lanes(jnp.where(lane < n_blocks, gate, NEG_INF), MOBA_TOPK)
        m_all = m_sc[...]
        s_own = lax.dot_general(qs, _pad_rows(knew_ref[0]).astype(BF16), (((1,), (1,)), ((), ())),
                                preferred_element_type=F32)
        s_own = jnp.where((lane <= row % t_new) & (lane < t_new), s_own, NEG_INF)
        m_own = jnp.max(s_own, axis=1, keepdims=True)
        p_own = jnp.exp(s_own - m_own)
        m_tot = jnp.maximum(jnp.max(jnp.where(sel, m_all, NEG_INF), axis=1, keepdims=True), m_own)
        wgt = jnp.where(sel, jnp.exp(m_all - m_tot), 0.0)
        w_own = jnp.exp(m_own - m_tot)
        l_tot = jnp.sum(wgt * l_sc[...], axis=1, keepdims=True) + w_own * jnp.sum(p_own, axis=1, keepdims=True)
        out = w_own * jnp.dot(p_own.astype(BF16), _pad_rows(vnew_ref[0]).astype(BF16), preferred_element_type=F32)
        for nb in range(n_blocks):
            out = out + wgt[:, nb:nb + 1] * acc_sc[nb]
        o_ref[0] = _head_diag_extract(out / l_tot, t_new)


def _page_specs(layer):
    def spec(r):
        return pl.BlockSpec((None, None, W_MIX, PAGE_SIZE),
                            lambda b, g, pt: (layer, pt[b, g * PAGES_PER_STEP + r], 0, 0))
    return [spec(r) for r in range(PAGES_PER_STEP)]


def _moba_sample(page_table, qa, knew, vnew, cache_k, cache_v, layer):
    nseq, t_new, _ = qa.shape
    n_pages = page_table.shape[1]
    past = n_pages * PAGE_SIZE
    assert past % MOBA_BLOCK == 0 and t_new <= MOBA_BLOCK and t_new % 8 == 0 and N_HEADS * t_new <= ROWS
    assert n_pages % PAGES_PER_STEP == 0 and past // MOBA_BLOCK <= LANES
    n_blocks = past // MOBA_BLOCK
    new_spec = pl.BlockSpec((1, t_new, W_MIX), lambda b, g, pt: (b, 0, 0))
    grid_spec = pltpu.PrefetchScalarGridSpec(
        num_scalar_prefetch=1,
        grid=(nseq, n_pages // PAGES_PER_STEP),
        in_specs=[new_spec] * 3 + _page_specs(layer) + _page_specs(layer),
        out_specs=new_spec,
        scratch_shapes=[pltpu.VMEM((ROWS, W_MIX), F32), pltpu.VMEM((ROWS, LANES), F32),
                        pltpu.VMEM((ROWS, LANES), F32), pltpu.VMEM((n_blocks, ROWS, W_MIX), F32),
                        pltpu.VMEM((W_MIX, LANES), F32)],
    )
    return pl.pallas_call(
        functools.partial(_moba_sample_kernel, n_blocks=n_blocks, t_new=t_new),
        grid_spec=grid_spec,
        out_shape=jax.ShapeDtypeStruct((nseq, t_new, W_MIX), F32),
        compiler_params=_cparams(("parallel", "arbitrary")),
        name="moba_sample",
    )(page_table, qa, knew, vnew, *([cache_k] * PAGES_PER_STEP), *([cache_v] * PAGES_PER_STEP))


SB_EAGER_PAGES = 2


def _sb_sample_kernel(pt_ref, q_ref, knew_ref, vnew_ref, k1_ref, k2_ref, v1_ref, v2_ref, kc_ref, vc_ref, tri_ref,
                      o_ref, kbuf, vbuf, sem, *, layer, n_pages, t_new):
    b = pl.program_id(0)
    lane = lax.broadcasted_iota(jnp.int32, (ROWS, LANES), 1)
    row = lax.broadcasted_iota(jnp.int32, (ROWS, LANES), 0)
    real_row = lax.broadcasted_iota(jnp.int32, (ROWS, 1), 0) < N_HEADS * t_new
    tri = tri_ref[...]
    qs = (_head_block_diag(q_ref[0]) * (1.0 / math.sqrt(HEAD_DIM))).astype(BF16)

    nt = (((1,), (1,)), ((), ()))

    def tile(kt, vt, carry, acc):
        z = jnp.dot(qs, kt.astype(BF16), preferred_element_type=F32)
        a, carry = _sb_tile(z, every, carry, tri)
        return carry, acc + lax.dot_general(a.astype(BF16), vt.astype(BF16), nt, preferred_element_type=F32)

    every = lane >= 0
    new_mask = (lane < row % t_new) & (lane < t_new)
    z = lax.dot_general(qs, _pad_rows(knew_ref[0]).astype(BF16), nt, preferred_element_type=F32)
    a, carry = _sb_tile(z, new_mask, jnp.zeros((ROWS, 1), F32), tri)
    acc = jnp.dot(a.astype(BF16), _pad_rows(vnew_ref[0]).astype(BF16), preferred_element_type=F32)
    carry, acc = tile(k1_ref[...], v1_ref[...], carry, acc)
    carry, acc = tile(k2_ref[...], v2_ref[...], carry, acc)

    def cond(state):
        p, carry, _ = state
        return (p >= 0) & (jnp.max(jnp.where(real_row, carry, NEG_INF)) > SB_UNDERFLOW)

    def body(state):
        p, carry, acc = state
        page = pt_ref[b, p]
        ck = pltpu.make_async_copy(kc_ref.at[layer, page], kbuf, sem.at[0])
        cv = pltpu.make_async_copy(vc_ref.at[layer, page], vbuf, sem.at[1])
        ck.start()
        cv.start()
        ck.wait()
        cv.wait()
        carry, acc = tile(kbuf[...], vbuf[...], carry, acc)
        return p - 1, carry, acc

    _, _, acc = lax.while_loop(cond, body, (jnp.int32(n_pages - SB_EAGER_PAGES - 1), carry, acc))
    o_ref[0] = _head_diag_extract(acc, t_new)


def _sb_sample(page_table, qb, knew, vnew, cache_k, cache_v, layer):
    nseq, t_new, _ = qb.shape
    n_pages = page_table.shape[1]
    assert n_pages >= SB_EAGER_PAGES and t_new % 8 == 0 and N_HEADS * t_new <= ROWS and t_new <= LANES
    new_spec = pl.BlockSpec((1, t_new, W_MIX), lambda b, pt: (b, 0, 0))
    page = lambda back: pl.BlockSpec((None, None, W_MIX, PAGE_SIZE),
                                     lambda b, pt: (layer, pt[b, n_pages - back], 0, 0))
    grid_spec = pltpu.PrefetchScalarGridSpec(
        num_scalar_prefetch=1,
        grid=(nseq,),
        in_specs=[new_spec] * 3 + [page(1), page(2), page(1), page(2),
                                   pl.BlockSpec(memory_space=pl.ANY), pl.BlockSpec(memory_space=pl.ANY),
                                   pl.BlockSpec((PAGE_SIZE, PAGE_SIZE), lambda b, pt: (0, 0))],
        out_specs=new_spec,
        scratch_shapes=[pltpu.VMEM((W_MIX, PAGE_SIZE), F32), pltpu.VMEM((W_MIX, PAGE_SIZE), F32),
                        pltpu.SemaphoreType.DMA((2,))],
    )
    return pl.pallas_call(
        functools.partial(_sb_sample_kernel, layer=layer, n_pages=n_pages, t_new=t_new),
        grid_spec=grid_spec,
        out_shape=jax.ShapeDtypeStruct((nseq, t_new, W_MIX), F32),
        compiler_params=_cparams(("parallel",)),
        name="sb_sample",
    )(page_table, qb, knew, vnew, cache_k, cache_k, cache_v, cache_v, cache_k, cache_v, _tri(PAGE_SIZE))


PEER_HALF = 128


def _postmix_kernel(x_ref, oa_ref, ob_ref, sga_ref, sgb_ref, wua_ref, wub_ref, wo_ref, gffn_ref, wpqt_ref, sk_ref,
                    h_ref, hnt_ref, s1t_ref, s2t_ref):
    ya = jnp.dot(oa_ref[...].astype(BF16), wua_ref[...], preferred_element_type=F32)
    yb = jnp.dot(ob_ref[...].astype(BF16), wub_ref[...], preferred_element_type=F32)
    merged = sga_ref[...] * ya + sgb_ref[...] * yb
    h = x_ref[...] + jnp.dot(merged.astype(BF16), wo_ref[...], preferred_element_type=F32)
    h_ref[...] = h
    ms = jnp.mean(h * h, axis=-1, keepdims=True)
    hn = (h * lax.rsqrt(ms + RMS_EPS)) * gffn_ref[...]
    hnt = hn.T.astype(BF16)
    hnt_ref[...] = hnt
    pqt = jnp.dot(wpqt_ref[...], hnt, preferred_element_type=F32)
    for hh in range(PEER_HEADS):
        r0 = hh * 2 * PEER_HALF
        s1t_ref[hh] = jnp.dot(sk_ref[hh, 0], pqt[r0:r0 + PEER_HALF].astype(BF16), preferred_element_type=F32)
        s2t_ref[hh] = jnp.dot(sk_ref[hh, 1], pqt[r0 + PEER_HALF:r0 + 2 * PEER_HALF].astype(BF16),
                              preferred_element_type=F32)


def _postmix(x, oa, ob, sga, sgb, wua16, wub16, wo16, gffn, wpqt16, sk16):
    t = x.shape[0]
    assert t % TOK_TILE == 0
    tile = lambda wd: pl.BlockSpec((TOK_TILE, wd), lambda i: (i, 0))
    score = pl.BlockSpec((PEER_HEADS, PEER_NKEYS, TOK_TILE), lambda i: (0, 0, i))
    return pl.pallas_call(
        _postmix_kernel,
        grid=(t // TOK_TILE,),
        in_specs=[tile(D_MODEL), tile(W_MIX), tile(W_MIX), tile(D_MODEL), tile(D_MODEL),
                  _resident(wua16.shape), _resident(wub16.shape), _resident(wo16.shape),
                  pl.BlockSpec((1, D_MODEL), lambda i: (0, 0)), _resident(wpqt16.shape), _resident(sk16.shape)],
        out_specs=[tile(D_MODEL), pl.BlockSpec((D_MODEL, TOK_TILE), lambda i: (0, i)), score, score],
        out_shape=[jax.ShapeDtypeStruct((t, D_MODEL), F32), jax.ShapeDtypeStruct((D_MODEL, t), BF16),
                   jax.ShapeDtypeStruct((PEER_HEADS, PEER_NKEYS, t), F32),
                   jax.ShapeDtypeStruct((PEER_HEADS, PEER_NKEYS, t), F32)],
        compiler_params=_cparams(("parallel",)),
        name="postmix",
    )(x, oa, ob, sga, sgb, wua16, wub16, wo16, gffn.reshape(1, -1), wpqt16, sk16)


ROUTE_LANES = 512
ROUTE_CHUNK = 128
NOT_RANKED = 127.0


def _top_rows_distinct(sa, sb, k):
    n, l = sa.shape
    kiota = lax.broadcasted_iota(jnp.int32, (k, l), 0)

    def step(r, s, vals):
        m = jnp.max(s, axis=0, keepdims=True)
        return jnp.where(s == m, -jnp.inf, s), jnp.where(kiota == r, m, vals)

    def body(r, carry):
        return step(r, *carry[:2]) + step(r, *carry[2:])

    left_a, vals_a, left_b, vals_b = lax.fori_loop(
        0, k, body, (sa, jnp.zeros((k, l), F32), sb, jnp.zeros((k, l), F32)))

    def ranks(s, left, vals):
        taken = left == -jnp.inf
        rank = jnp.zeros((n, l), jnp.int32)
        for i in range(k):
            rank = rank + jnp.where(vals[i:i + 1] > s, 1, 0)
        count = jnp.sum(jnp.where(taken, 1, 0), axis=0, keepdims=True)
        return jnp.where(taken, rank, k), count

    rank_a, count_a = ranks(sa, left_a, vals_a)
    rank_b, count_b = ranks(sb, left_b, vals_b)
    distinct = jnp.min(jnp.where((count_a == k) & (count_b == k), 1, 0)) == 1
    return rank_a, vals_a, rank_b, vals_b, distinct


def _top_rows_pair(sa, sb, k):
    n, l = sa.shape
    riota = lax.broadcasted_iota(jnp.int32, (n, l), 0)
    kiota = lax.broadcasted_iota(jnp.int32, (k, l), 0)

    def step(r, s, rank, vals):
        m = jnp.max(s, axis=0, keepdims=True)
        pick = riota == jnp.min(jnp.where(s == m, riota, n), axis=0, keepdims=True)
        return jnp.where(pick, -jnp.inf, s), jnp.where(pick, r, rank), jnp.where(kiota == r, m, vals)

    def body(r, carry):
        return step(r, *carry[:3]) + step(r, *carry[3:])

    init = (jnp.full((n, l), k, jnp.int32), jnp.zeros((k, l), F32))
    out = lax.fori_loop(0, k, body, (sa,) + init + (sb,) + init)
    return out[1], out[2], out[4], out[5]


def _pair_candidates(v1, v2, k):
    assert k == 16
    l = v1.shape[1]
    parts = [v1[0:1] + v2, v1[1:2] + v2[0:8]] + [v1[i:i + 1] + v2[0:8] for i in range(2, 8)] + [v1[8:16] + v2[0:1]]
    cand = jnp.concatenate(parts, axis=0)
    r = lax.broadcasted_iota(jnp.int32, cand.shape, 0)
    grp = r // 8
    last = cand.shape[0] // 8 - 1
    flat = jnp.where(grp <= 2, r, jnp.where(grp == last, (r - 8 * (last - 1)) * k, r + 8 * grp - k))
    return cand, flat


def _route_kernel(s1_ref, s2_ref, na_ref, wa_ref, rb_ref, e2_ref, rank_sc, vals_sc):
    k = PEER_TOPK

    def chunk(c, _):
        sl = pl.ds(pl.multiple_of(c * ROUTE_CHUNK, ROUTE_CHUNK), ROUTE_CHUNK)
        s1 = s1_ref[0, :, sl]
        s2 = s2_ref[0, :, sl]

        rank_sc[0], vals_sc[0], rank_sc[1], vals_sc[1], distinct = _top_rows_distinct(s1, s2, k)

        @pl.when(jnp.logical_not(distinct))
        def _():
            rank_sc[0], vals_sc[0], rank_sc[1], vals_sc[1] = _top_rows_pair(s1, s2, k)

        rank1, v1, rank2, v2 = rank_sc[0], vals_sc[0], rank_sc[1], vals_sc[1]
        cand, flat = _pair_candidates(v1, v2, k)
        l = cand.shape[1]
        kiota = lax.broadcasted_iota(jnp.int32, (k, l), 0)
        top = cand[0:1]

        def body(r, carry):
            cnd, cnt, z = carry
            m = jnp.max(cnd, axis=0, keepdims=True)
            idx = jnp.min(jnp.where(cnd == m, flat, k * k), axis=0, keepdims=True)
            cnt = jnp.where(kiota == idx // k, cnt + 1.0, cnt)
            return jnp.where(flat == idx, -jnp.inf, cnd), cnt, z + jnp.exp(m - top)

        _, cnt, z = lax.fori_loop(0, k, body, (cand, jnp.zeros((k, l), F32), jnp.zeros((1, l), F32)))
        na = jnp.zeros(s1.shape, F32)
        for i in range(k):
            na = jnp.where(rank1 == i, cnt[i:i + 1], na)
        na_ref[0, :, sl] = na
        wa_ref[0, :, sl] = jnp.where(rank1 < k, jnp.exp(s1 - v1[0:1]) / z, 0.0)
        rb_ref[0, :, sl] = jnp.where(rank2 < k, rank2.astype(F32), NOT_RANKED).astype(BF16)
        e2_ref[0, :, sl] = jnp.where(rank2 < k, jnp.exp(s2 - v2[0:1]), 0.0).astype(BF16)
        return 0

    lax.fori_loop(0, ROUTE_LANES // ROUTE_CHUNK, chunk, 0)


def _route(s1t, s2t):
    t = s1t.shape[2]
    assert t % ROUTE_LANES == 0
    blk = pl.BlockSpec((1, PEER_NKEYS, ROUTE_LANES), lambda i, h: (h, 0, i))
    f32 = jax.ShapeDtypeStruct(s1t.shape, F32)
    b16 = jax.ShapeDtypeStruct(s1t.shape, BF16)
    return pl.pallas_call(
        _route_kernel,
        grid=(t // ROUTE_LANES, PEER_HEADS),
        in_specs=[blk, blk],
        out_specs=[blk] * 4,
        out_shape=[f32, f32, b16, b16],
        scratch_shapes=[pltpu.VMEM((2, PEER_NKEYS, ROUTE_CHUNK), jnp.int32),
                        pltpu.VMEM((2, PEER_TOPK, ROUTE_CHUNK), F32)],
        compiler_params=_cparams(("parallel", "parallel")),
        name="peer_route",
    )(s1t, s2t)


PEER_TOK = 512
A_PER_STEP = 8
BF16_SUBLANES = 16


def _peer_kernel(hnt_ref, na_ref, wa_ref, rb_ref, e2_ref, u_ref, v_ref, o_ref, acc_sc):
    j = pl.program_id(1)

    @pl.when(j == 0)
    def _():
        acc_sc[...] = jnp.zeros_like(acc_sc)

    hnt = hnt_ref[...]
    tok = hnt.shape[1]

    def row16(ref, h, aa):
        slab = jnp.broadcast_to(ref[h, aa:aa + 1, :], (BF16_SUBLANES, tok)).astype(BF16)
        return jnp.concatenate([slab] * (PEER_NKEYS // BF16_SUBLANES), axis=0)

    parts = []
    for aa in range(A_PER_STEP):
        act = jnp.dot(u_ref[aa * PEER_NKEYS:(aa + 1) * PEER_NKEYS, :], hnt, preferred_element_type=F32)
        gel = 0.5 * act * (1.0 + lax.erf(act * (1.0 / math.sqrt(2.0))))
        gate = jnp.zeros(act.shape, BF16)
        for h in range(PEER_HEADS):
            gate = gate + jnp.where(rb_ref[h] < row16(na_ref, h, aa), e2_ref[h], 0.0) * row16(wa_ref, h, aa)
        parts.append(gate * gel.astype(BF16))
    pt = jnp.concatenate(parts, axis=0)
    acc_sc[...] += lax.dot_general(pt, v_ref[...], (((0,), (0,)), ((), ())), preferred_element_type=F32)

    @pl.when(j == pl.num_programs(1) - 1)
    def _():
        o_ref[...] = acc_sc[...]


def _peer(hnt, na, wa, rb, e2, u16, v16):
    t = hnt.shape[1]
    assert t % PEER_TOK == 0 and u16.shape[0] == PEER_NKEYS * PEER_NKEYS
    n_exp = A_PER_STEP * PEER_NKEYS
    route = pl.BlockSpec((PEER_HEADS, PEER_NKEYS, PEER_TOK), lambda i, j: (0, 0, i))
    a_rows = pl.BlockSpec((PEER_HEADS, A_PER_STEP, PEER_TOK), lambda i, j: (0, j, i))
    table = pl.BlockSpec((n_exp, D_MODEL), lambda i, j: (j, 0))
    return pl.pallas_call(
        _peer_kernel,
        grid=(t // PEER_TOK, PEER_NKEYS // A_PER_STEP),
        in_specs=[pl.BlockSpec((D_MODEL, PEER_TOK), lambda i, j: (0, i)), a_rows, a_rows, route, route, table, table],
        out_specs=pl.BlockSpec((PEER_TOK, D_MODEL), lambda i, j: (i, 0)),
        out_shape=jax.ShapeDtypeStruct((t, D_MODEL), F32),
        scratch_shapes=[pltpu.VMEM((PEER_TOK, D_MODEL), F32)],
        compiler_params=_cparams(("parallel", "arbitrary")),
        name="peer_experts",
    )(hnt, na, wa, rb, e2, u16, v16)


def _ple_kernel(h_ref, f_ref, p_ref, gple_ref, wg_ref, wp_ref, y_ref):
    h = h_ref[...] + f_ref[...]
    ms = jnp.mean(h * h, axis=-1, keepdims=True)
    hn = ((h * lax.rsqrt(ms + RMS_EPS)) * gple_ref[...]).astype(BF16)
    gate = jax.nn.sigmoid(jnp.dot(hn, wg_ref[...], preferred_element_type=F32))
    y_ref[...] = h + gate * jnp.dot(p_ref[...].astype(BF16), wp_ref[...], preferred_element_type=F32)


def _ple(h, ffn, p, gple, wg16, wp16):
    t = h.shape[0]
    assert t % TOK_TILE == 0
    tile = lambda wd: pl.BlockSpec((TOK_TILE, wd), lambda i: (i, 0))
    return pl.pallas_call(
        _ple_kernel,
        grid=(t // TOK_TILE,),
        in_specs=[tile(D_MODEL), tile(D_MODEL), tile(p.shape[1]), pl.BlockSpec((1, D_MODEL), lambda i: (0, 0)),
                  _resident(wg16.shape), _resident(wp16.shape)],
        out_specs=tile(D_MODEL),
        out_shape=jax.ShapeDtypeStruct((t, D_MODEL), F32),
        compiler_params=_cparams(("parallel",)),
        name="ple_gate",
    )(h, ffn, p, gple.reshape(1, -1), wg16, wp16)


def _post_mix(x, oa, ob, sga, sgb, p, wts):
    h, hnt, s1t, s2t = _postmix(x, oa, ob, sga, sgb, wts["wua"], wts["wub"], wts["wo"], wts["gffn"],
                                wts["wpqt"], wts["sk"])
    na, wa, rb, e2 = _route(s1t, s2t)
    ffn = _peer(hnt, na, wa, rb, e2, wts["u"], wts["v"])
    return _ple(h, ffn, p, wts["gple"], wts["wg"], wts["wp"])


def kernel(x_prompt, x_sample, cache_k_moba, cache_v_moba, cache_k_sb, cache_v_sb, page_table, p_prompt, p_sample,
           norm_mix, w_in, q_norm_moba, k_norm_moba, w_up_moba, w_up_sb, w_o, norm_ffn, w_peer_q, peer_sub_keys,
           peer_u, peer_v, norm_ple, w_ple_gate, w_ple):
    depth = w_in.shape[0]
    nb, s, _ = x_prompt.shape
    nseq, t_new, _ = x_sample.shape
    past = page_table.shape[1] * PAGE_SIZE
    pos_p = jnp.arange(s, dtype=jnp.int32)
    pos_s = jnp.tile(past + jnp.arange(t_new, dtype=jnp.int32), nseq)
    paged = lambda c: jnp.transpose(c, (0, 1, 3, 4, 2)).reshape(c.shape[0], c.shape[1], W_MIX, PAGE_SIZE)
    ckm, cvm, cks, cvs = paged(cache_k_moba), paged(cache_v_moba), paged(cache_k_sb), paged(cache_v_sb)
    heads = lambda a, lead: a.reshape(*lead, N_HEADS, HEAD_DIM)
    heads_t = lambda a: jnp.transpose(a.reshape(N_HEADS, HEAD_DIM, -1), (2, 0, 1))

    hp = [x_prompt[b] for b in range(nb)]
    hs = x_sample.reshape(nseq * t_new, D_MODEL)
    kv_p = [[] for _ in range(4)]
    kv_s = [[] for _ in range(4)]
    for l in range(depth):
        w_in16 = w_in[l].astype(BF16)
        wts = dict(wua=w_up_moba[l].astype(BF16), wub=w_up_sb[l].astype(BF16), wo=w_o[l].astype(BF16),
                   gffn=norm_ffn[l], wpqt=w_peer_q[l].T.astype(BF16), sk=peer_sub_keys[l].astype(BF16),
                   u=peer_u[l].astype(BF16), v=peer_v[l].astype(BF16), gple=norm_ple[l],
                   wg=w_ple_gate[l].astype(BF16), wp=w_ple[l].astype(BF16))
        layer_kv = [[] for _ in range(4)]
        for b in range(nb):
            (qa, ka, va, qb, kb, vb, sga, sgb, ka16, va16, kb16, vb16, kmean) = _inproj(
                hp[b], pos_p, norm_mix[l], w_in16, q_norm_moba[l], k_norm_moba[l], True)
            oa = _moba_prompt(qa, ka16, va16, kmean.reshape(-1, W_MIX))
            ob = _sb_prompt(qb, kb16, vb16)
            hp[b] = _post_mix(hp[b], oa, ob, sga, sgb, p_prompt[l, b], wts)
            for dst, a in zip(layer_kv, (ka, va, kb, vb)):
                dst.append(heads_t(a))
        for dst, src in zip(kv_p, layer_kv):
            dst.append(jnp.stack(src))

        (qa, ka, va, qb, kb, vb, sga, sgb, *_) = _inproj(hs, pos_s, norm_mix[l], w_in16, q_norm_moba[l],
                                                          k_norm_moba[l], False)
        seq = lambda a: a.reshape(nseq, t_new, W_MIX)
        oa = _moba_sample(page_table, seq(qa), seq(ka), seq(va), ckm, cvm, l)
        ob = _sb_sample(page_table, seq(qb), seq(kb), seq(vb), cks, cvs, l)
        hs = _post_mix(hs, oa.reshape(-1, W_MIX), ob.reshape(-1, W_MIX), sga, sgb,
                       p_sample[l].reshape(-1, p_sample.shape[-1]), wts)
        for dst, a in zip(kv_s, (ka, va, kb, vb)):
            dst.append(heads(a, (nseq, t_new)))

    return (jnp.stack(hp), hs.reshape(nseq, t_new, D_MODEL),
            *[jnp.stack(a) for a in kv_p], *[jnp.stack(a) for a in kv_s])
```

```python
import functools
import math

import jax
import jax.numpy as jnp
from jax import lax
from jax.experimental import pallas as pl
from jax.experimental.pallas import tpu as pltpu

F32 = jnp.float32
BF16 = jnp.bfloat16

D_MODEL = 1024
HEAD_DIM = 64
N_HEADS = 8
W_MIX = N_HEADS * HEAD_DIM
PAGE_SIZE = 128
MOBA_BLOCK = 256
MOBA_TOPK = 3
ROPE_THETA = 10000.0
PEER_HEADS = 8
PEER_NKEYS = 128
PEER_TOPK = 16
RMS_EPS = 1e-6
NEG_INF = -1e30
LANES = 128
SB_UNDERFLOW = -104.0
VMEM_LIMIT = 56 * 1024 * 1024

TOK_TILE = 512
Q_TILE = 256


def _cparams(sem):
    return pltpu.CompilerParams(dimension_semantics=sem, vmem_limit_bytes=VMEM_LIMIT)


def _resident(shape):
    nd = len(shape)
    return pl.BlockSpec(shape, lambda *_: (0,) * nd, pipeline_mode=pl.Buffered(1))


def _split_dot(x, m):
    hi = x.astype(BF16)
    lo = (x - hi.astype(F32)).astype(BF16)
    return (jnp.dot(hi, m, preferred_element_type=F32) + jnp.dot(lo, m, preferred_element_type=F32))


def _inproj_kernel(x_ref, gmix_ref, w_ref, qg_ref, kg_ref, cos_ref, sin_ref, bavg_ref,
                   qa_ref, ka_ref, va_ref, qb_ref, kb_ref, vb_ref, sga_ref, sgb_ref,
                   ka16_ref, va16_ref, kb16_ref, vb16_ref, kmean_ref, *, kv_tokens_on_lanes):
    kv_out = (lambda a: a.T) if kv_tokens_on_lanes else (lambda a: a)
    x = x_ref[...]
    ms = jnp.mean(x * x, axis=-1, keepdims=True)
    xn = ((x * lax.rsqrt(ms + RMS_EPS)) * gmix_ref[...]).astype(BF16)

    def proj(c0, c1):
        return jnp.dot(xn, w_ref[:, c0:c1], preferred_element_type=F32)

    cos = jnp.concatenate([cos_ref[...]] * (W_MIX // LANES), axis=1)
    sin = jnp.concatenate([sin_ref[...]] * (W_MIX // LANES), axis=1)
    lane = lax.broadcasted_iota(jnp.int32, (x.shape[0], W_MIX), 1)
    low_half = (lane % HEAD_DIM) < (HEAD_DIM // 2)
    bavg = bavg_ref[...]

    def headnorm_rope(z, gain):
        msq = _split_dot(z * z, bavg)
        y = (z * lax.rsqrt(msq + RMS_EPS)) * gain
        partner = jnp.where(low_half, pltpu.roll(y, W_MIX - HEAD_DIM // 2, 1),
                            pltpu.roll(y, HEAD_DIM // 2, 1))
        return y * cos + partner * sin

    w = W_MIX
    qa_ref[...] = headnorm_rope(proj(0, w), qg_ref[...])
    ka = headnorm_rope(proj(w, 2 * w), kg_ref[...])
    ka_ref[...] = kv_out(ka)
    ka16_ref[...] = ka.astype(BF16)
    nblk = ka.shape[0] // MOBA_BLOCK
    kmean_ref[0] = jnp.mean(ka.reshape(nblk, MOBA_BLOCK, W_MIX), axis=1)
    va = proj(2 * w, 3 * w)
    va_ref[...] = kv_out(va)
    va16_ref[...] = kv_out(va).astype(BF16)
    qb_ref[...] = proj(3 * w, 4 * w)
    kb = proj(4 * w, 5 * w)
    kb_ref[...] = kv_out(kb)
    kb16_ref[...] = kb.astype(BF16)
    vb = proj(5 * w, 6 * w)
    vb_ref[...] = kv_out(vb)
    vb16_ref[...] = vb.astype(BF16)
    sga_ref[...] = jax.nn.sigmoid(proj(6 * w, 6 * w + D_MODEL))
    sgb_ref[...] = jax.nn.sigmoid(proj(6 * w + D_MODEL, 6 * w + 2 * D_MODEL))


def _rope_tables(pos):
    half = HEAD_DIM // 2
    inv = ROPE_THETA ** (-jnp.arange(half, dtype=F32) / half)
    ang = pos.astype(F32)[:, None] * inv[None, :]
    cos, sin = jnp.cos(ang), jnp.sin(ang)
    reps = LANES // HEAD_DIM
    return (jnp.concatenate([cos, cos] * reps, axis=1), jnp.concatenate([-sin, sin] * reps, axis=1))


def _inproj(x, pos, gmix, w_in16, qgain, kgain, kv_tokens_on_lanes):
    t = x.shape[0]
    assert t % TOK_TILE == 0
    nt = t // TOK_TILE
    cos, sin = _rope_tables(pos)
    head = jnp.arange(W_MIX) // HEAD_DIM
    bavg = jnp.where(head[:, None] == head[None, :], 1.0 / HEAD_DIM, 0.0).astype(BF16)
    tile = lambda wd: pl.BlockSpec((TOK_TILE, wd), lambda i: (i, 0))
    row = lambda wd: pl.BlockSpec((1, wd), lambda i: (0, 0))
    f32_w = jax.ShapeDtypeStruct((t, W_MIX), F32)
    b16_w = jax.ShapeDtypeStruct((t, W_MIX), BF16)
    f32_d = jax.ShapeDtypeStruct((t, D_MODEL), F32)
    nblk = TOK_TILE // MOBA_BLOCK
    if kv_tokens_on_lanes:
        kv_spec = pl.BlockSpec((W_MIX, TOK_TILE), lambda i: (0, i))
        kv_shape = jax.ShapeDtypeStruct((W_MIX, t), F32)
        va16_shape = jax.ShapeDtypeStruct((W_MIX, t), BF16)
    else:
        kv_spec, kv_shape, va16_shape = tile(W_MIX), f32_w, b16_w
    return pl.pallas_call(
        functools.partial(_inproj_kernel, kv_tokens_on_lanes=kv_tokens_on_lanes),
        grid=(nt,),
        in_specs=[tile(D_MODEL), row(D_MODEL), _resident(w_in16.shape), row(W_MIX), row(W_MIX),
                  tile(LANES), tile(LANES), _resident((W_MIX, W_MIX))],
        out_specs=[tile(W_MIX), kv_spec, kv_spec] * 2 + [tile(D_MODEL)] * 2
                  + [tile(W_MIX), kv_spec, tile(W_MIX), tile(W_MIX)]
                  + [pl.BlockSpec((1, nblk, W_MIX), lambda i: (i, 0, 0))],
        out_shape=[f32_w, kv_shape, kv_shape] * 2 + [f32_d] * 2 + [b16_w, va16_shape, b16_w, b16_w]
                  + [jax.ShapeDtypeStruct((nt, nblk, W_MIX), F32)],
        compiler_params=_cparams(("parallel",)),
        name="inproj",
    )(x, gmix.reshape(1, -1), w_in16, jnp.tile(qgain, N_HEADS).reshape(1, -1),
      jnp.tile(kgain, N_HEADS).reshape(1, -1), cos, sin, bavg)


def _top_mask_lanes(g, count):
    lane = lax.broadcasted_iota(jnp.int32, g.shape, 1)
    sel = jnp.zeros(g.shape, jnp.bool_)
    for _ in range(count):
        m = jnp.max(g, axis=1, keepdims=True)
        idx = jnp.min(jnp.where(g == m, lane, g.shape[1]), axis=1, keepdims=True)
        pick = (lane == idx) & (m > NEG_INF / 2)
        sel = sel | pick
        g = jnp.where(pick, NEG_INF, g)
    return sel


LOG2_E = math.log2(math.e)


def _top_mask_rows(g, count):
    riota = lax.broadcasted_iota(jnp.int32, g.shape, 0)
    sel = jnp.zeros(g.shape, jnp.bool_)
    for _ in range(count):
        m = jnp.max(g, axis=0, keepdims=True)
        idx = jnp.min(jnp.where(g == m, riota, g.shape[0]), axis=0, keepdims=True)
        pick = (riota == idx) & (m > NEG_INF / 2)
        sel = sel | pick
        g = jnp.where(pick, NEG_INF, g)
    return sel


def _moba_prompt_kernel(q_ref, k_ref, vt_ref, kmean_ref, o_ref, wq_sc, sel_sc, m_sc, l_sc, acc_sc, st_sc):
    i = pl.program_id(0)
    tq = Q_TILE
    npairs = N_HEADS // 2
    nbp = kmean_ref.shape[0]
    qt = q_ref[...].T
    blk = lax.broadcasted_iota(jnp.int32, (nbp, tq), 0)
    prow = lax.broadcasted_iota(jnp.int32, (LANES, tq), 0)
    klane = lax.broadcasted_iota(jnp.int32, (nbp, LANES), 1)
    key = lax.broadcasted_iota(jnp.int32, (MOBA_BLOCK, tq), 0)
    qry = lax.broadcasted_iota(jnp.int32, (MOBA_BLOCK, tq), 1)
    own0 = pl.multiple_of(i * MOBA_BLOCK, MOBA_BLOCK)

    def head_rows(h):
        return slice(h * HEAD_DIM, (h + 1) * HEAD_DIM)

    for p in range(npairs):
        qp = qt[p * LANES:(p + 1) * LANES]
        km = kmean_ref[:, p * LANES:(p + 1) * LANES]
        halves = []
        for w in range(2):
            h = 2 * p + w
            gate = jnp.dot(jnp.where(klane // HEAD_DIM == w, km, 0.0), qp, preferred_element_type=F32,
                           precision=lax.Precision.HIGHEST)
            sel = _top_mask_rows(jnp.where(blk < i, gate, NEG_INF), MOBA_TOPK)
            sel_sc[h] = jnp.where(sel, 1.0, 0.0)
            halves.append(jnp.where(prow // HEAD_DIM == w, qp * (LOG2_E / math.sqrt(HEAD_DIM)), 0.0).astype(BF16))
        wq = jnp.concatenate(halves, axis=1)
        wq_sc[p] = wq
        st = jnp.dot(k_ref[pl.ds(own0, MOBA_BLOCK), p * LANES:(p + 1) * LANES], wq, preferred_element_type=F32)
        for w in range(2):
            h = 2 * p + w
            s = jnp.where(key <= qry, st[:, w * tq:(w + 1) * tq], NEG_INF)
            m = jnp.max(s, axis=0, keepdims=True)
            pt = jnp.exp2(s - m)
            m_sc[h:h + 1, :] = m
            l_sc[h:h + 1, :] = jnp.sum(pt, axis=0, keepdims=True)
            acc_sc[head_rows(h), :] = jnp.dot(vt_ref[head_rows(h), pl.ds(own0, MOBA_BLOCK)], pt.astype(BF16),
                                              preferred_element_type=F32)

    def scores(n, slot):
        r0 = pl.multiple_of(n * MOBA_BLOCK, MOBA_BLOCK)
        for p in range(npairs):
            st_sc[slot, p] = jnp.dot(k_ref[pl.ds(r0, MOBA_BLOCK), p * LANES:(p + 1) * LANES], wq_sc[p],
                                     preferred_element_type=F32)

    def attend(n, slot):
        r0 = pl.multiple_of(n * MOBA_BLOCK, MOBA_BLOCK)
        for h in range(N_HEADS):
            s = st_sc[slot, h // 2, :, (h % 2) * tq:(h % 2 + 1) * tq]
            chosen = sel_sc[h, pl.ds(n, 1), :] > 0.0
            m_old = m_sc[h:h + 1, :]
            m_new = jnp.maximum(m_old, jnp.where(chosen, jnp.max(s, axis=0, keepdims=True), NEG_INF))
            alpha = jnp.exp2(m_old - m_new)
            pt = jnp.exp2(s - jnp.where(chosen, m_new, -NEG_INF))
            m_sc[h:h + 1, :] = m_new
            l_sc[h:h + 1, :] = alpha * l_sc[h:h + 1, :] + jnp.sum(pt, axis=0, keepdims=True)
            acc_sc[head_rows(h), :] = alpha * acc_sc[head_rows(h), :] + jnp.dot(
                vt_ref[head_rows(h), pl.ds(r0, MOBA_BLOCK)], pt.astype(BF16), preferred_element_type=F32)

    scores(0, 0)

    def body(k, _):
        n = 2 * k
        scores(n + 1, 1)
        attend(n, 0)
        scores(jnp.minimum(n + 2, i), 0)
        attend(n + 1, 1)
        return 0

    lax.fori_loop(0, (i + 1) // 2, body, 0)
    out_t = acc_sc[...].reshape(N_HEADS, HEAD_DIM, tq) / l_sc[...][:, None, :]
    o_ref[...] = out_t.reshape(W_MIX, tq).T


def _moba_prompt(qa, ka16, vat16, kmean):
    s = qa.shape[0]
    assert s % Q_TILE == 0 and Q_TILE == MOBA_BLOCK
    nb = kmean.shape[0]
    kmean = jnp.pad(kmean, ((0, -nb % 8), (0, 0)))
    return pl.pallas_call(
        _moba_prompt_kernel,
        grid=(s // Q_TILE,),
        in_specs=[pl.BlockSpec((Q_TILE, W_MIX), lambda i: (i, 0)), _resident(ka16.shape), _resident(vat16.shape),
                  _resident(kmean.shape)],
        out_specs=pl.BlockSpec((Q_TILE, W_MIX), lambda i: (i, 0)),
        out_shape=jax.ShapeDtypeStruct((s, W_MIX), F32),
        scratch_shapes=[pltpu.VMEM((N_HEADS // 2, LANES, 2 * Q_TILE), BF16),
                        pltpu.VMEM((N_HEADS, kmean.shape[0], Q_TILE), F32),
                        pltpu.VMEM((N_HEADS, Q_TILE), F32), pltpu.VMEM((N_HEADS, Q_TILE), F32),
                        pltpu.VMEM((W_MIX, Q_TILE), F32),
                        pltpu.VMEM((2, N_HEADS // 2, MOBA_BLOCK, 2 * Q_TILE), F32)],
        compiler_params=_cparams(("parallel",)),
        name="moba_prompt",
    )(qa, ka16, vat16, kmean)


def _sb_tile(z, mask, carry, tri):
    t = jnp.log1p(jnp.exp(-jnp.abs(z)))
    log_beta = jnp.minimum(z, 0.0) - t
    log_keep = jnp.where(mask, -jnp.maximum(z, 0.0) - t, 0.0)
    after = _split_dot(log_keep, tri) + carry
    a = jnp.where(mask, jnp.exp(log_beta + after), 0.0)
    return a, carry + jnp.sum(log_keep, axis=1, keepdims=True)


def _sb_prompt_kernel(q_ref, k_ref, v_ref, tri_ref, o_ref, acc_sc):
    i = pl.program_id(0)
    tq = Q_TILE
    lane = lax.broadcasted_iota(jnp.int32, (tq, LANES), 1)
    row = lax.broadcasted_iota(jnp.int32, (tq, tq), 0)
    col = lax.broadcasted_iota(jnp.int32, (tq, tq), 1)
    tri = tri_ref[...]
    nt = (((1,), (1,)), ((), ()))
    qes = []
    for h in range(N_HEADS):
        q2 = q_ref[:, (h // 2) * LANES:(h // 2 + 1) * LANES]
        qes.append(jnp.where(lane // HEAD_DIM == h % 2, q2 * (1.0 / math.sqrt(HEAD_DIM)), 0.0).astype(BF16))
    acc_sc[...] = jnp.zeros_like(acc_sc)

    def cond(state):
        j, carries = state
        worst = carries[0]
        for c in carries[1:]:
            worst = jnp.maximum(worst, c)
        return (j >= 0) & (jnp.max(worst) > SB_UNDERFLOW)

    def body(state):
        j, carries = state
        r0 = pl.multiple_of(j * tq, tq)
        mask = (col + (j - i) * tq) < row
        pair = lambda ref, h: ref[pl.ds(r0, tq), (h // 2) * LANES:(h // 2 + 1) * LANES]
        zs = [lax.dot_general(qes[h], pair(k_ref, h), nt, preferred_element_type=F32) for h in range(N_HEADS)]
        log_betas, afters, new_carries = [], [], []
        for h in range(N_HEADS):
            t = jnp.log1p(jnp.exp(-jnp.abs(zs[h])))
            log_betas.append(jnp.minimum(zs[h], 0.0) - t)
            log_keep = jnp.where(mask, -jnp.maximum(zs[h], 0.0) - t, 0.0)
            afters.append(_split_dot(log_keep, tri) + carries[h])
            new_carries.append(carries[h] + jnp.sum(log_keep, axis=1, keepdims=True))
        pvs = []
        for h in range(N_HEADS):
            a = jnp.where(mask, jnp.exp(log_betas[h] + afters[h]), 0.0)
            pvs.append(jnp.dot(a.astype(BF16), pair(v_ref, h), preferred_element_type=F32))
        for p in range(N_HEADS // 2):
            acc_sc[:, p * LANES:(p + 1) * LANES] += jnp.where(lane < HEAD_DIM, pvs[2 * p], pvs[2 * p + 1])
        return j - 1, tuple(new_carries)

    lax.while_loop(cond, body, (i, tuple(jnp.zeros((tq, 1), F32) for _ in range(N_HEADS))))
    o_ref[...] = acc_sc[...]


def _tri(n):
    r = jnp.arange(n)
    return (r[:, None] > r[None, :]).astype(BF16)


def _sb_prompt(qb, kb16, vb16):
    s = qb.shape[0]
    assert s % Q_TILE == 0
    return pl.pallas_call(
        _sb_prompt_kernel,
        grid=(s // Q_TILE,),
        in_specs=[pl.BlockSpec((Q_TILE, W_MIX), lambda i: (i, 0)), _resident(kb16.shape), _resident(vb16.shape),
                  _resident((Q_TILE, Q_TILE))],
        out_specs=pl.BlockSpec((Q_TILE, W_MIX), lambda i: (i, 0)),
        out_shape=jax.ShapeDtypeStruct((s, W_MIX), F32),
        scratch_shapes=[pltpu.VMEM((Q_TILE, W_MIX), F32)],
        compiler_params=_cparams(("parallel",)),
        name="sb_prompt",
    )(qb, kb16, vb16, _tri(Q_TILE))


PAGES_PER_STEP = 8
ROWS = LANES


def _head_block_diag(q):
    t = q.shape[0]
    lanehead = lax.broadcasted_iota(jnp.int32, q.shape, 1) // HEAD_DIM
    parts = [jnp.where(lanehead == h, q, 0.0) for h in range(N_HEADS)]
    parts.append(jnp.zeros((ROWS - N_HEADS * t, q.shape[1]), F32))
    return jnp.concatenate(parts, axis=0)


def _head_diag_extract(full, t):
    lanehead = lax.broadcasted_iota(jnp.int32, (t, full.shape[1]), 1) // HEAD_DIM
    out = jnp.zeros((t, full.shape[1]), F32)
    for h in range(N_HEADS):
        out = out + jnp.where(lanehead == h, full[h * t:(h + 1) * t], 0.0)
    return out


def _pad_rows(x):
    return jnp.concatenate([x, jnp.zeros((ROWS - x.shape[0], x.shape[1]), x.dtype)], axis=0)


def _moba_sample_kernel(pt_ref, q_ref, knew_ref, vnew_ref, *rest, n_blocks, t_new):
    kp = rest[:PAGES_PER_STEP]
    vp = rest[PAGES_PER_STEP:2 * PAGES_PER_STEP]
    o_ref, qbd_sc, m_sc, l_sc, acc_sc, kmean_sc = rest[2 * PAGES_PER_STEP:]
    g = pl.program_id(1)
    lane = lax.broadcasted_iota(jnp.int32, (ROWS, LANES), 1)
    row = lax.broadcasted_iota(jnp.int32, (ROWS, LANES), 0)

    @pl.when(g == 0)
    def _():
        qbd_sc[...] = _head_block_diag(q_ref[0])
        m_sc[...] = jnp.zeros_like(m_sc)
        l_sc[...] = jnp.zeros_like(l_sc)
        kmean_sc[...] = jnp.zeros_like(kmean_sc)

    qbd = qbd_sc[...]
    qs = (qbd * (1.0 / math.sqrt(HEAD_DIM))).astype(BF16)
    pages_per_block = MOBA_BLOCK // PAGE_SIZE
    blocks_per_step = PAGES_PER_STEP // pages_per_block
    klane = lax.broadcasted_iota(jnp.int32, kmean_sc.shape, 1)
    m_new, l_new, km_new = m_sc[...], l_sc[...], kmean_sc[...]
    kts = [jnp.concatenate([kp[jj * pages_per_block + r][...] for r in range(pages_per_block)], axis=1)
           for jj in range(blocks_per_step)]
    scores = [jnp.dot(qs, kt.astype(BF16), preferred_element_type=F32) for kt in kts]
    for jj in range(blocks_per_step):
        n = g * blocks_per_step + jj
        kt, s = kts[jj], scores[jj]
        vt = jnp.concatenate([vp[jj * pages_per_block + r][...] for r in range(pages_per_block)], axis=1)
        m = jnp.max(s, axis=1, keepdims=True)
        p = jnp.exp(s - m)
        m_new = jnp.where(lane == n, m, m_new)
        l_new = jnp.where(lane == n, jnp.sum(p, axis=1, keepdims=True), l_new)
        acc_sc[n] = lax.dot_general(p.astype(BF16), vt.astype(BF16), (((1,), (1,)), ((), ())),
                                    preferred_element_type=F32)
        km_new = jnp.where(klane == n, jnp.mean(kt, axis=1, keepdims=True), km_new)
    m_sc[...] = m_new
    l_sc[...] = l_new
    kmean_sc[...] = km_new

    @pl.when(g == pl.num_programs(1) - 1)
    def _():
        gate = jnp.dot(qbd, kmean_sc[...], preferred_element_type=F32, precision=lax.Precision.HIGHEST)
        sel = _top_mask_lanes(jnp.where(lane < n_blocks, gate, NEG_INF), MOBA_TOPK)
        m_all = m_sc[...]
        s_own = lax.dot_general(qs, _pad_rows(knew_ref[0]).astype(BF16), (((1,), (1,)), ((), ())),
                                preferred_element_type=F32)
        s_own = jnp.where((lane <= row % t_new) & (lane < t_new), s_own, NEG_INF)
        m_own = jnp.max(s_own, axis=1, keepdims=True)
        p_own = jnp.exp(s_own - m_own)
        m_tot = jnp.maximum(jnp.max(jnp.where(sel, m_all, NEG_INF), axis=1, keepdims=True), m_own)
        wgt = jnp.where(sel, jnp.exp(m_all - m_tot), 0.0)
        w_own = jnp.exp(m_own - m_tot)
        l_tot = jnp.sum(wgt * l_sc[...], axis=1, keepdims=True) + w_own * jnp.sum(p_own, axis=1, keepdims=True)
        out = w_own * jnp.dot(p_own.astype(BF16), _pad_rows(vnew_ref[0]).astype(BF16), preferred_element_type=F32)
        for nb in range(n_blocks):
            out = out + wgt[:, nb:nb + 1] * acc_sc[nb]
        o_ref[0] = _head_diag_extract(out / l_tot, t_new)


def _page_specs(layer):
    def spec(r):
        return pl.BlockSpec((None, None, W_MIX, PAGE_SIZE),
                            lambda b, g, pt: (layer, pt[b, g * PAGES_PER_STEP + r], 0, 0))
    return [spec(r) for r in range(PAGES_PER_STEP)]


def _moba_sample(page_table, qa, knew, vnew, cache_k, cache_v, layer):
    nseq, t_new, _ = qa.shape
    n_pages = page_table.shape[1]
    past = n_pages * PAGE_SIZE
    assert past % MOBA_BLOCK == 0 and t_new <= MOBA_BLOCK and t_new % 8 == 0 and N_HEADS * t_new <= ROWS
    assert n_pages % PAGES_PER_STEP == 0 and past // MOBA_BLOCK <= LANES
    n_blocks = past // MOBA_BLOCK
    new_spec = pl.BlockSpec((1, t_new, W_MIX), lambda b, g, pt: (b, 0, 0))
    grid_spec = pltpu.PrefetchScalarGridSpec(
        num_scalar_prefetch=1,
        grid=(nseq, n_pages // PAGES_PER_STEP),
        in_specs=[new_spec] * 3 + _page_specs(layer) + _page_specs(layer),
        out_specs=new_spec,
        scratch_shapes=[pltpu.VMEM((ROWS, W_MIX), F32), pltpu.VMEM((ROWS, LANES), F32),
                        pltpu.VMEM((ROWS, LANES), F32), pltpu.VMEM((n_blocks, ROWS, W_MIX), F32),
                        pltpu.VMEM((W_MIX, LANES), F32)],
    )
    return pl.pallas_call(
        functools.partial(_moba_sample_kernel, n_blocks=n_blocks, t_new=t_new),
        grid_spec=grid_spec,
        out_shape=jax.ShapeDtypeStruct((nseq, t_new, W_MIX), F32),
        compiler_params=_cparams(("parallel", "arbitrary")),
        name="moba_sample",
    )(page_table, qa, knew, vnew, *([cache_k] * PAGES_PER_STEP), *([cache_v] * PAGES_PER_STEP))


SB_EAGER_PAGES = 2


def _sb_sample_kernel(pt_ref, q_ref, knew_ref, vnew_ref, k1_ref, k2_ref, v1_ref, v2_ref, kc_ref, vc_ref, tri_ref,
                      o_ref, kbuf, vbuf, sem, *, layer, n_pages, t_new):
    b = pl.program_id(0)
    lane = lax.broadcasted_iota(jnp.int32, (ROWS, LANES), 1)
    row = lax.broadcasted_iota(jnp.int32, (ROWS, LANES), 0)
    real_row = lax.broadcasted_iota(jnp.int32, (ROWS, 1), 0) < N_HEADS * t_new
    tri = tri_ref[...]
    qs = (_head_block_diag(q_ref[0]) * (1.0 / math.sqrt(HEAD_DIM))).astype(BF16)

    nt = (((1,), (1,)), ((), ()))

    def tile(kt, vt, carry, acc):
        z = jnp.dot(qs, kt.astype(BF16), preferred_element_type=F32)
        a, carry = _sb_tile(z, every, carry, tri)
        return carry, acc + lax.dot_general(a.astype(BF16), vt.astype(BF16), nt, preferred_element_type=F32)

    every = lane >= 0
    new_mask = (lane < row % t_new) & (lane < t_new)
    z = lax.dot_general(qs, _pad_rows(knew_ref[0]).astype(BF16), nt, preferred_element_type=F32)
    a, carry = _sb_tile(z, new_mask, jnp.zeros((ROWS, 1), F32), tri)
    acc = jnp.dot(a.astype(BF16), _pad_rows(vnew_ref[0]).astype(BF16), preferred_element_type=F32)
    carry, acc = tile(k1_ref[...], v1_ref[...], carry, acc)
    carry, acc = tile(k2_ref[...], v2_ref[...], carry, acc)

    def cond(state):
        p, carry, _ = state
        return (p >= 0) & (jnp.max(jnp.where(real_row, carry, NEG_INF)) > SB_UNDERFLOW)

    def body(state):
        p, carry, acc = state
        page = pt_ref[b, p]
        ck = pltpu.make_async_copy(kc_ref.at[layer, page], kbuf, sem.at[0])
        cv = pltpu.make_async_copy(vc_ref.at[layer, page], vbuf, sem.at[1])
        ck.start()
        cv.start()
        ck.wait()
        cv.wait()
        carry, acc = tile(kbuf[...], vbuf[...], carry, acc)
        return p - 1, carry, acc

    _, _, acc = lax.while_loop(cond, body, (jnp.int32(n_pages - SB_EAGER_PAGES - 1), carry, acc))
    o_ref[0] = _head_diag_extract(acc, t_new)


def _sb_sample(page_table, qb, knew, vnew, cache_k, cache_v, layer):
    nseq, t_new, _ = qb.shape
    n_pages = page_table.shape[1]
    assert n_pages >= SB_EAGER_PAGES and t_new % 8 == 0 and N_HEADS * t_new <= ROWS and t_new <= LANES
    new_spec = pl.BlockSpec((1, t_new, W_MIX), lambda b, pt: (b, 0, 0))
    page = lambda back: pl.BlockSpec((None, None, W_MIX, PAGE_SIZE),
                                     lambda b, pt: (layer, pt[b, n_pages - back], 0, 0))
    grid_spec = pltpu.PrefetchScalarGridSpec(
        num_scalar_prefetch=1,
        grid=(nseq,),
        in_specs=[new_spec] * 3 + [page(1), page(2), page(1), page(2),
                                   pl.BlockSpec(memory_space=pl.ANY), pl.BlockSpec(memory_space=pl.ANY),
                                   pl.BlockSpec((PAGE_SIZE, PAGE_SIZE), lambda b, pt: (0, 0))],
        out_specs=new_spec,
        scratch_shapes=[pltpu.VMEM((W_MIX, PAGE_SIZE), F32), pltpu.VMEM((W_MIX, PAGE_SIZE), F32),
                        pltpu.SemaphoreType.DMA((2,))],
    )
    return pl.pallas_call(
        functools.partial(_sb_sample_kernel, layer=layer, n_pages=n_pages, t_new=t_new),
        grid_spec=grid_spec,
        out_shape=jax.ShapeDtypeStruct((nseq, t_new, W_MIX), F32),
        compiler_params=_cparams(("parallel",)),
        name="sb_sample",
    )(page_table, qb, knew, vnew, cache_k, cache_k, cache_v, cache_v, cache_k, cache_v, _tri(PAGE_SIZE))


PEER_HALF = 128


def _postmix_kernel(x_ref, oa_ref, ob_ref, sga_ref, sgb_ref, wua_ref, wub_ref, wo_ref, gffn_ref, wpqt_ref, sk_ref,
                    h_ref, hnt_ref, s1t_ref, s2t_ref):
    ya = jnp.dot(oa_ref[...].astype(BF16), wua_ref[...], preferred_element_type=F32)
    yb = jnp.dot(ob_ref[...].astype(BF16), wub_ref[...], preferred_element_type=F32)
    merged = sga_ref[...] * ya + sgb_ref[...] * yb
    h = x_ref[...] + jnp.dot(merged.astype(BF16), wo_ref[...], preferred_element_type=F32)
    h_ref[...] = h
    ms = jnp.mean(h * h, axis=-1, keepdims=True)
    hn = (h * lax.rsqrt(ms + RMS_EPS)) * gffn_ref[...]
    hnt = hn.T.astype(BF16)
    hnt_ref[...] = hnt
    pqt = jnp.dot(wpqt_ref[...], hnt, preferred_element_type=F32)
    for hh in range(PEER_HEADS):
        r0 = hh * 2 * PEER_HALF
        s1t_ref[hh] = jnp.dot(sk_ref[hh, 0], pqt[r0:r0 + PEER_HALF].astype(BF16), preferred_element_type=F32)
        s2t_ref[hh] = jnp.dot(sk_ref[hh, 1], pqt[r0 + PEER_HALF:r0 + 2 * PEER_HALF].astype(BF16),
                              preferred_element_type=F32)


def _postmix(x, oa, ob, sga, sgb, wua16, wub16, wo16, gffn, wpqt16, sk16):
    t = x.shape[0]
    assert t % TOK_TILE == 0
    tile = lambda wd: pl.BlockSpec((TOK_TILE, wd), lambda i: (i, 0))
    score = pl.BlockSpec((PEER_HEADS, PEER_NKEYS, TOK_TILE), lambda i: (0, 0, i))
    return pl.pallas_call(
        _postmix_kernel,
        grid=(t // TOK_TILE,),
        in_specs=[tile(D_MODEL), tile(W_MIX), tile(W_MIX), tile(D_MODEL), tile(D_MODEL),
                  _resident(wua16.shape), _resident(wub16.shape), _resident(wo16.shape),
                  pl.BlockSpec((1, D_MODEL), lambda i: (0, 0)), _resident(wpqt16.shape), _resident(sk16.shape)],
        out_specs=[tile(D_MODEL), pl.BlockSpec((D_MODEL, TOK_TILE), lambda i: (0, i)), score, score],
        out_shape=[jax.ShapeDtypeStruct((t, D_MODEL), F32), jax.ShapeDtypeStruct((D_MODEL, t), BF16),
                   jax.ShapeDtypeStruct((PEER_HEADS, PEER_NKEYS, t), F32),
                   jax.ShapeDtypeStruct((PEER_HEADS, PEER_NKEYS, t), F32)],
        compiler_params=_cparams(("parallel",)),
        name="postmix",
    )(x, oa, ob, sga, sgb, wua16, wub16, wo16, gffn.reshape(1, -1), wpqt16, sk16)


ROUTE_LANES = 512
ROUTE_CHUNK = 128
NOT_RANKED = 127.0


def _top_rows_distinct(sa, sb, k):
    n, l = sa.shape
    kiota = lax.broadcasted_iota(jnp.int32, (k, l), 0)

    def step(r, s, vals):
        m = jnp.max(s, axis=0, keepdims=True)
        return jnp.where(s == m, -jnp.inf, s), jnp.where(kiota == r, m, vals)

    def body(r, carry):
        return step(r, *carry[:2]) + step(r, *carry[2:])

    left_a, vals_a, left_b, vals_b = lax.fori_loop(
        0, k, body, (sa, jnp.zeros((k, l), F32), sb, jnp.zeros((k, l), F32)))

    def ranks(s, left, vals):
        taken = left == -jnp.inf
        rank = jnp.zeros((n, l), jnp.int32)
        for i in range(k):
            rank = rank + jnp.where(vals[i:i + 1] > s, 1, 0)
        count = jnp.sum(jnp.where(taken, 1, 0), axis=0, keepdims=True)
        return jnp.where(taken, rank, k), count

    rank_a, count_a = ranks(sa, left_a, vals_a)
    rank_b, count_b = ranks(sb, left_b, vals_b)
    distinct = jnp.min(jnp.where((count_a == k) & (count_b == k), 1, 0)) == 1
    return rank_a, vals_a, rank_b, vals_b, distinct


def _top_rows_pair(sa, sb, k):
    n, l = sa.shape
    riota = lax.broadcasted_iota(jnp.int32, (n, l), 0)
    kiota = lax.broadcasted_iota(jnp.int32, (k, l), 0)

    def step(r, s, rank, vals):
        m = jnp.max(s, axis=0, keepdims=True)
        pick = riota == jnp.min(jnp.where(s == m, riota, n), axis=0, keepdims=True)
        return jnp.where(pick, -jnp.inf, s), jnp.where(pick, r, rank), jnp.where(kiota == r, m, vals)

    def body(r, carry):
        return step(r, *carry[:3]) + step(r, *carry[3:])

    init = (jnp.full((n, l), k, jnp.int32), jnp.zeros((k, l), F32))
    out = lax.fori_loop(0, k, body, (sa,) + init + (sb,) + init)
    return out[1], out[2], out[4], out[5]


def _pair_candidates(v1, v2, k):
    assert k == 16
    l = v1.shape[1]
    parts = [v1[0:1] + v2, v1[1:2] + v2[0:8]] + [v1[i:i + 1] + v2[0:8] for i in range(2, 8)] + [v1[8:16] + v2[0:1]]
    cand = jnp.concatenate(parts, axis=0)
    r = lax.broadcasted_iota(jnp.int32, cand.shape, 0)
    grp = r // 8
    last = cand.shape[0] // 8 - 1
    flat = jnp.where(grp <= 2, r, jnp.where(grp == last, (r - 8 * (last - 1)) * k, r + 8 * grp - k))
    return cand, flat


def _route_kernel(s1_ref, s2_ref, na_ref, wa_ref, rb_ref, e2_ref, rank_sc, vals_sc):
    k = PEER_TOPK

    def chunk(c, _):
        sl = pl.ds(pl.multiple_of(c * ROUTE_CHUNK, ROUTE_CHUNK), ROUTE_CHUNK)
        s1 = s1_ref[0, :, sl]
        s2 = s2_ref[0, :, sl]

        rank_sc[0], vals_sc[0], rank_sc[1], vals_sc[1], distinct = _top_rows_distinct(s1, s2, k)

        @pl.when(jnp.logical_not(distinct))
        def _():
            rank_sc[0], vals_sc[0], rank_sc[1], vals_sc[1] = _top_rows_pair(s1, s2, k)

        rank1, v1, rank2, v2 = rank_sc[0], vals_sc[0], rank_sc[1], vals_sc[1]
        cand, flat = _pair_candidates(v1, v2, k)
        l = cand.shape[1]
        kiota = lax.broadcasted_iota(jnp.int32, (k, l), 0)
        top = cand[0:1]

        def body(r, carry):
            cnd, cnt, z = carry
            m = jnp.max(cnd, axis=0, keepdims=True)
            idx = jnp.min(jnp.where(cnd == m, flat, k * k), axis=0, keepdims=True)
            cnt = jnp.where(kiota == idx // k, cnt + 1.0, cnt)
            return jnp.where(flat == idx, -jnp.inf, cnd), cnt, z + jnp.exp(m - top)

        _, cnt, z = lax.fori_loop(0, k, body, (cand, jnp.zeros((k, l), F32), jnp.zeros((1, l), F32)))
        na = jnp.zeros(s1.shape, F32)
        for i in range(k):
            na = jnp.where(rank1 == i, cnt[i:i + 1], na)
        na_ref[0, :, sl] = na
        wa_ref[0, :, sl] = jnp.where(rank1 < k, jnp.exp(s1 - v1[0:1]) / z, 0.0)
        rb_ref[0, :, sl] = jnp.where(rank2 < k, rank2.astype(F32), NOT_RANKED).astype(BF16)
        e2_ref[0, :, sl] = jnp.where(rank2 < k, jnp.exp(s2 - v2[0:1]), 0.0).astype(BF16)
        return 0

    lax.fori_loop(0, ROUTE_LANES // ROUTE_CHUNK, chunk, 0)


def _route(s1t, s2t):
    t = s1t.shape[2]
    assert t % ROUTE_LANES == 0
    blk = pl.BlockSpec((1, PEER_NKEYS, ROUTE_LANES), lambda i, h: (h, 0, i))
    f32 = jax.ShapeDtypeStruct(s1t.shape, F32)
    b16 = jax.ShapeDtypeStruct(s1t.shape, BF16)
    return pl.pallas_call(
        _route_kernel,
        grid=(t // ROUTE_LANES, PEER_HEADS),
        in_specs=[blk, blk],
        out_specs=[blk] * 4,
        out_shape=[f32, f32, b16, b16],
        scratch_shapes=[pltpu.VMEM((2, PEER_NKEYS, ROUTE_CHUNK), jnp.int32),
                        pltpu.VMEM((2, PEER_TOPK, ROUTE_CHUNK), F32)],
        compiler_params=_cparams(("parallel", "parallel")),
        name="peer_route",
    )(s1t, s2t)


PEER_TOK = 512
A_PER_STEP = 8
BF16_SUBLANES = 16


def _peer_kernel(hnt_ref, na_ref, wa_ref, rb_ref, e2_ref, u_ref, v_ref, o_ref, acc_sc):
    j = pl.program_id(1)

    @pl.when(j == 0)
    def _():
        acc_sc[...] = jnp.zeros_like(acc_sc)

    hnt = hnt_ref[...]
    tok = hnt.shape[1]

    def row16(ref, h, aa):
        slab = jnp.broadcast_to(ref[h, aa:aa + 1, :], (BF16_SUBLANES, tok)).astype(BF16)
        return jnp.concatenate([slab] * (PEER_NKEYS // BF16_SUBLANES), axis=0)

    parts = []
    for aa in range(A_PER_STEP):
        act = jnp.dot(u_ref[aa * PEER_NKEYS:(aa + 1) * PEER_NKEYS, :], hnt, preferred_element_type=F32)
        gel = 0.5 * act * (1.0 + lax.erf(act * (1.0 / math.sqrt(2.0))))
        gate = jnp.zeros(act.shape, BF16)
        for h in range(PEER_HEADS):
            gate = gate + jnp.where(rb_ref[h] < row16(na_ref, h, aa), e2_ref[h], 0.0) * row16(wa_ref, h, aa)
        parts.append(gate * gel.astype(BF16))
    pt = jnp.concatenate(parts, axis=0)
    acc_sc[...] += lax.dot_general(pt, v_ref[...], (((0,), (0,)), ((), ())), preferred_element_type=F32)

    @pl.when(j == pl.num_programs(1) - 1)
    def _():
        o_ref[...] = acc_sc[...]


def _peer(hnt, na, wa, rb, e2, u16, v16):
    t = hnt.shape[1]
    assert t % PEER_TOK == 0 and u16.shape[0] == PEER_NKEYS * PEER_NKEYS
    n_exp = A_PER_STEP * PEER_NKEYS
    route = pl.BlockSpec((PEER_HEADS, PEER_NKEYS, PEER_TOK), lambda i, j: (0, 0, i))
    a_rows = pl.BlockSpec((PEER_HEADS, A_PER_STEP, PEER_TOK), lambda i, j: (0, j, i))
    table = pl.BlockSpec((n_exp, D_MODEL), lambda i, j: (j, 0))
    return pl.pallas_call(
        _peer_kernel,
        grid=(t // PEER_TOK, PEER_NKEYS // A_PER_STEP),
        in_specs=[pl.BlockSpec((D_MODEL, PEER_TOK), lambda i, j: (0, i)), a_rows, a_rows, route, route, table, table],
        out_specs=pl.BlockSpec((PEER_TOK, D_MODEL), lambda i, j: (i, 0)),
        out_shape=jax.ShapeDtypeStruct((t, D_MODEL), F32),
        scratch_shapes=[pltpu.VMEM((PEER_TOK, D_MODEL), F32)],
        compiler_params=_cparams(("parallel", "arbitrary")),
        name="peer_experts",
    )(hnt, na, wa, rb, e2, u16, v16)


def _ple_kernel(h_ref, f_ref, p_ref, gple_ref, wg_ref, wp_ref, y_ref):
    h = h_ref[...] + f_ref[...]
    ms = jnp.mean(h * h, axis=-1, keepdims=True)
    hn = ((h * lax.rsqrt(ms + RMS_EPS)) * gple_ref[...]).astype(BF16)
    gate = jax.nn.sigmoid(jnp.dot(hn, wg_ref[...], preferred_element_type=F32))
    y_ref[...] = h + gate * jnp.dot(p_ref[...].astype(BF16), wp_ref[...], preferred_element_type=F32)


def _ple(h, ffn, p, gple, wg16, wp16):
    t = h.shape[0]
    assert t % TOK_TILE == 0
    tile = lambda wd: pl.BlockSpec((TOK_TILE, wd), lambda i: (i, 0))
    return pl.pallas_call(
        _ple_kernel,
        grid=(t // TOK_TILE,),
        in_specs=[tile(D_MODEL), tile(D_MODEL), tile(p.shape[1]), pl.BlockSpec((1, D_MODEL), lambda i: (0, 0)),
                  _resident(wg16.shape), _resident(wp16.shape)],
        out_specs=tile(D_MODEL),
        out_shape=jax.ShapeDtypeStruct((t, D_MODEL), F32),
        compiler_params=_cparams(("parallel",)),
        name="ple_gate",
    )(h, ffn, p, gple.reshape(1, -1), wg16, wp16)


def _post_mix(x, oa, ob, sga, sgb, p, wts):
    h, hnt, s1t, s2t = _postmix(x, oa, ob, sga, sgb, wts["wua"], wts["wub"], wts["wo"], wts["gffn"],
                                wts["wpqt"], wts["sk"])
    na, wa, rb, e2 = _route(s1t, s2t)
    ffn = _peer(hnt, na, wa, rb, e2, wts["u"], wts["v"])
    return _ple(h, ffn, p, wts["gple"], wts["wg"], wts["wp"])


def kernel(x_prompt, x_sample, cache_k_moba, cache_v_moba, cache_k_sb, cache_v_sb, page_table, p_prompt, p_sample,
           norm_mix, w_in, q_norm_moba, k_norm_moba, w_up_moba, w_up_sb, w_o, norm_ffn, w_peer_q, peer_sub_keys,
           peer_u, peer_v, norm_ple, w_ple_gate, w_ple):
    depth = w_in.shape[0]
    nb, s, _ = x_prompt.shape
    nseq, t_new, _ = x_sample.shape
    past = page_table.shape[1] * PAGE_SIZE
    pos_p = jnp.arange(s, dtype=jnp.int32)
    pos_s = jnp.tile(past + jnp.arange(t_new, dtype=jnp.int32), nseq)
    paged = lambda c: jnp.transpose(c, (0, 1, 3, 4, 2)).reshape(c.shape[0], c.shape[1], W_MIX, PAGE_SIZE)
    ckm, cvm, cks, cvs = paged(cache_k_moba), paged(cache_v_moba), paged(cache_k_sb), paged(cache_v_sb)
    heads = lambda a, lead: a.reshape(*lead, N_HEADS, HEAD_DIM)
    heads_t = lambda a: jnp.transpose(a.reshape(N_HEADS, HEAD_DIM, -1), (2, 0, 1))

    hp = [x_prompt[b] for b in range(nb)]
    hs = x_sample.reshape(nseq * t_new, D_MODEL)
    kv_p = [[] for _ in range(4)]
    kv_s = [[] for _ in range(4)]
    for l in range(depth):
        w_in16 = w_in[l].astype(BF16)
        wts = dict(wua=w_up_moba[l].astype(BF16), wub=w_up_sb[l].astype(BF16), wo=w_o[l].astype(BF16),
                   gffn=norm_ffn[l], wpqt=w_peer_q[l].T.astype(BF16), sk=peer_sub_keys[l].astype(BF16),
                   u=peer_u[l].astype(BF16), v=peer_v[l].astype(BF16), gple=norm_ple[l],
                   wg=w_ple_gate[l].astype(BF16), wp=w_ple[l].astype(BF16))
        layer_kv = [[] for _ in range(4)]
        for b in range(nb):
            (qa, ka, va, qb, kb, vb, sga, sgb, ka16, va16, kb16, vb16, kmean) = _inproj(
                hp[b], pos_p, norm_mix[l], w_in16, q_norm_moba[l], k_norm_moba[l], True)
            oa = _moba_prompt(qa, ka16, va16, kmean.reshape(-1, W_MIX))
            ob = _sb_prompt(qb, kb16, vb16)
            hp[b] = _post_mix(hp[b], oa, ob, sga, sgb, p_prompt[l, b], wts)
            for dst, a in zip(layer_kv, (ka, va, kb, vb)):
                dst.append(heads_t(a))
        for dst, src in zip(kv_p, layer_kv):
            dst.append(jnp.stack(src))

        (qa, ka, va, qb, kb, vb, sga, sgb, *_) = _inproj(hs, pos_s, norm_mix[l], w_in16, q_norm_moba[l],
                                                          k_norm_moba[l], False)
        seq = lambda a: a.reshape(nseq, t_new, W_MIX)
        oa = _moba_sample(page_table, seq(qa), seq(ka), seq(va), ckm, cvm, l)
        ob = _sb_sample(page_table, seq(qb), seq(kb), seq(vb), cks, cvs, l)
        hs = _post_mix(hs, oa.reshape(-1, W_MIX), ob.reshape(-1, W_MIX), sga, sgb,
                       p_sample[l].reshape(-1, p_sample.shape[-1]), wts)
        for dst, a in zip(kv_s, (ka, va, kb, vb)):
            dst.append(heads(a, (nseq, t_new)))

    return (jnp.stack(hp), hs.reshape(nseq, t_new, D_MODEL),
            *[jnp.stack(a) for a in kv_p], *[jnp.stack(a) for a in kv_s])
```

```python
import functools
import math

import jax
import jax.numpy as jnp
from jax import lax
from jax.experimental import pallas as pl
from jax.experimental.pallas import tpu as pltpu

F32 = jnp.float32
BF16 = jnp.bfloat16

D_MODEL = 1024
HEAD_DIM = 64
N_HEADS = 8
W_MIX = N_HEADS * HEAD_DIM
PAGE_SIZE = 128
MOBA_BLOCK = 256
MOBA_TOPK = 3
ROPE_THETA = 10000.0
PEER_HEADS = 8
PEER_NKEYS = 128
PEER_TOPK = 16
RMS_EPS = 1e-6
NEG_INF = -1e30
LANES = 128
SB_UNDERFLOW = -104.0
VMEM_LIMIT = 56 * 1024 * 1024

TOK_TILE = 512
Q_TILE = 256


def _cparams(sem):
    return pltpu.CompilerParams(dimension_semantics=sem, vmem_limit_bytes=VMEM_LIMIT)


def _resident(shape):
    nd = len(shape)
    return pl.BlockSpec(shape, lambda *_: (0,) * nd, pipeline_mode=pl.Buffered(1))


def _split_dot(x, m):
    hi = x.astype(BF16)
    lo = (x - hi.astype(F32)).astype(BF16)
    return (jnp.dot(hi, m, preferred_element_type=F32) + jnp.dot(lo, m, preferred_element_type=F32))


def _inproj_kernel(x_ref, gmix_ref, w_ref, qg_ref, kg_ref, cos_ref, sin_ref, bavg_ref,
                   qa_ref, ka_ref, va_ref, qb_ref, kb_ref, vb_ref, sga_ref, sgb_ref,
                   ka16_ref, va16_ref, kb16_ref, vb16_ref, kmean_ref, *, kv_tokens_on_lanes):
    kv_out = (lambda a: a.T) if kv_tokens_on_lanes else (lambda a: a)
    x = x_ref[...]
    ms = jnp.mean(x * x, axis=-1, keepdims=True)
    xn = ((x * lax.rsqrt(ms + RMS_EPS)) * gmix_ref[...]).astype(BF16)

    def proj(c0, c1):
        return jnp.dot(xn, w_ref[:, c0:c1], preferred_element_type=F32)

    cos = jnp.concatenate([cos_ref[...]] * (W_MIX // LANES), axis=1)
    sin = jnp.concatenate([sin_ref[...]] * (W_MIX // LANES), axis=1)
    lane = lax.broadcasted_iota(jnp.int32, (x.shape[0], W_MIX), 1)
    low_half = (lane % HEAD_DIM) < (HEAD_DIM // 2)
    bavg = bavg_ref[...]

    def headnorm_rope(z, gain):
        msq = _split_dot(z * z, bavg)
        y = (z * lax.rsqrt(msq + RMS_EPS)) * gain
        partner = jnp.where(low_half, pltpu.roll(y, W_MIX - HEAD_DIM // 2, 1),
                            pltpu.roll(y, HEAD_DIM // 2, 1))
        return y * cos + partner * sin

    w = W_MIX
    qa_ref[...] = headnorm_rope(proj(0, w), qg_ref[...])
    ka = headnorm_rope(proj(w, 2 * w), kg_ref[...])
    ka_ref[...] = kv_out(ka)
    ka16_ref[...] = ka.astype(BF16)
    nblk = ka.shape[0] // MOBA_BLOCK
    kmean_ref[0] = jnp.mean(ka.reshape(nblk, MOBA_BLOCK, W_MIX), axis=1)
    va = proj(2 * w, 3 * w)
    va_ref[...] = kv_out(va)
    va16_ref[...] = kv_out(va).astype(BF16)
    qb_ref[...] = proj(3 * w, 4 * w)
    kb = proj(4 * w, 5 * w)
    kb_ref[...] = kv_out(kb)
    kb16_ref[...] = kb.astype(BF16)
    vb = proj(5 * w, 6 * w)
    vb_ref[...] = kv_out(vb)
    vb16_ref[...] = vb.astype(BF16)
    sga_ref[...] = jax.nn.sigmoid(proj(6 * w, 6 * w + D_MODEL))
    sgb_ref[...] = jax.nn.sigmoid(proj(6 * w + D_MODEL, 6 * w + 2 * D_MODEL))


def _rope_tables(pos):
    half = HEAD_DIM // 2
    inv = ROPE_THETA ** (-jnp.arange(half, dtype=F32) / half)
    ang = pos.astype(F32)[:, None] * inv[None, :]
    cos, sin = jnp.cos(ang), jnp.sin(ang)
    reps = LANES // HEAD_DIM
    return (jnp.concatenate([cos, cos] * reps, axis=1), jnp.concatenate([-sin, sin] * reps, axis=1))


def _inproj(x, pos, gmix, w_in16, qgain, kgain, kv_tokens_on_lanes):
    t = x.shape[0]
    assert t % TOK_TILE == 0
    nt = t // TOK_TILE
    cos, sin = _rope_tables(pos)
    head = jnp.arange(W_MIX) // HEAD_DIM
    bavg = jnp.where(head[:, None] == head[None, :], 1.0 / HEAD_DIM, 0.0).astype(BF16)
    tile = lambda wd: pl.BlockSpec((TOK_TILE, wd), lambda i: (i, 0))
    row = lambda wd: pl.BlockSpec((1, wd), lambda i: (0, 0))
    f32_w = jax.ShapeDtypeStruct((t, W_MIX), F32)
    b16_w = jax.ShapeDtypeStruct((t, W_MIX), BF16)
    f32_d = jax.ShapeDtypeStruct((t, D_MODEL), F32)
    nblk = TOK_TILE // MOBA_BLOCK
    if kv_tokens_on_lanes:
        kv_spec = pl.BlockSpec((W_MIX, TOK_TILE), lambda i: (0, i))
        kv_shape = jax.ShapeDtypeStruct((W_MIX, t), F32)
        va16_shape = jax.ShapeDtypeStruct((W_MIX, t), BF16)
    else:
        kv_spec, kv_shape, va16_shape = tile(W_MIX), f32_w, b16_w
    return pl.pallas_call(
        functools.partial(_inproj_kernel, kv_tokens_on_lanes=kv_tokens_on_lanes),
        grid=(nt,),
        in_specs=[tile(D_MODEL), row(D_MODEL), _resident(w_in16.shape), row(W_MIX), row(W_MIX),
                  tile(LANES), tile(LANES), _resident((W_MIX, W_MIX))],
        out_specs=[tile(W_MIX), kv_spec, kv_spec] * 2 + [tile(D_MODEL)] * 2
                  + [tile(W_MIX), kv_spec, tile(W_MIX), tile(W_MIX)]
                  + [pl.BlockSpec((1, nblk, W_MIX), lambda i: (i, 0, 0))],
        out_shape=[f32_w, kv_shape, kv_shape] * 2 + [f32_d] * 2 + [b16_w, va16_shape, b16_w, b16_w]
                  + [jax.ShapeDtypeStruct((nt, nblk, W_MIX), F32)],
        compiler_params=_cparams(("parallel",)),
        name="inproj",
    )(x, gmix.reshape(1, -1), w_in16, jnp.tile(qgain, N_HEADS).reshape(1, -1),
      jnp.tile(kgain, N_HEADS).reshape(1, -1), cos, sin, bavg)


def _top_mask_lanes(g, count):
    lane = lax.broadcasted_iota(jnp.int32, g.shape, 1)
    sel = jnp.zeros(g.shape, jnp.bool_)
    for _ in range(count):
        m = jnp.max(g, axis=1, keepdims=True)
        idx = jnp.min(jnp.where(g == m, lane, g.shape[1]), axis=1, keepdims=True)
        pick = (lane == idx) & (m > NEG_INF / 2)
        sel = sel | pick
        g = jnp.where(pick, NEG_INF, g)
    return sel


LOG2_E = math.log2(math.e)


def _top_mask_rows(g, count):
    riota = lax.broadcasted_iota(jnp.int32, g.shape, 0)
    sel = jnp.zeros(g.shape, jnp.bool_)
    for _ in range(count):
        m = jnp.max(g, axis=0, keepdims=True)
        idx = jnp.min(jnp.where(g == m, riota, g.shape[0]), axis=0, keepdims=True)
        pick = (riota == idx) & (m > NEG_INF / 2)
        sel = sel | pick
        g = jnp.where(pick, NEG_INF, g)
    return sel


def _moba_prompt_kernel(q_ref, k_ref, vt_ref, kmean_ref, o_ref, wq_sc, sel_sc, m_sc, l_sc, acc_sc, st_sc):
    i = pl.program_id(0)
    tq = Q_TILE
    npairs = N_HEADS // 2
    nbp = kmean_ref.shape[0]
    qt = q_ref[...].T
    blk = lax.broadcasted_iota(jnp.int32, (nbp, tq), 0)
    prow = lax.broadcasted_iota(jnp.int32, (LANES, tq), 0)
    klane = lax.broadcasted_iota(jnp.int32, (nbp, LANES), 1)
    key = lax.broadcasted_iota(jnp.int32, (MOBA_BLOCK, tq), 0)
    qry = lax.broadcasted_iota(jnp.int32, (MOBA_BLOCK, tq), 1)
    own0 = pl.multiple_of(i * MOBA_BLOCK, MOBA_BLOCK)

    def head_rows(h):
        return slice(h * HEAD_DIM, (h + 1) * HEAD_DIM)

    for p in range(npairs):
        qp = qt[p * LANES:(p + 1) * LANES]
        km = kmean_ref[:, p * LANES:(p + 1) * LANES]
        halves = []
        for w in range(2):
            h = 2 * p + w
            gate = jnp.dot(jnp.where(klane // HEAD_DIM == w, km, 0.0), qp, preferred_element_type=F32,
                           precision=lax.Precision.HIGHEST)
            sel = _top_mask_rows(jnp.where(blk < i, gate, NEG_INF), MOBA_TOPK)
            sel_sc[h] = jnp.where(sel, 1.0, 0.0)
            halves.append(jnp.where(prow // HEAD_DIM == w, qp * (LOG2_E / math.sqrt(HEAD_DIM)), 0.0).astype(BF16))
        wq = jnp.concatenate(halves, axis=1)
        wq_sc[p] = wq
        st = jnp.dot(k_ref[pl.ds(own0, MOBA_BLOCK), p * LANES:(p + 1) * LANES], wq, preferred_element_type=F32)
        for w in range(2):
            h = 2 * p + w
            s = jnp.where(key <= qry, st[:, w * tq:(w + 1) * tq], NEG_INF)
            m = jnp.max(s, axis=0, keepdims=True)
            pt = jnp.exp2(s - m)
            m_sc[h:h + 1, :] = m
            l_sc[h:h + 1, :] = jnp.sum(pt, axis=0, keepdims=True)
            acc_sc[head_rows(h), :] = jnp.dot(vt_ref[head_rows(h), pl.ds(own0, MOBA_BLOCK)], pt.astype(BF16),
                                              preferred_element_type=F32)

    def scores(n, slot):
        r0 = pl.multiple_of(n * MOBA_BLOCK, MOBA_BLOCK)
        for p in range(npairs):
            st_sc[slot, p] = jnp.dot(k_ref[pl.ds(r0, MOBA_BLOCK), p * LANES:(p + 1) * LANES], wq_sc[p],
                                     preferred_element_type=F32)

    def attend(n, slot):
        r0 = pl.multiple_of(n * MOBA_BLOCK, MOBA_BLOCK)
        for h in range(N_HEADS):
            s = st_sc[slot, h // 2, :, (h % 2) * tq:(h % 2 + 1) * tq]
            chosen = sel_sc[h, pl.ds(n, 1), :] > 0.0
            m_old = m_sc[h:h + 1, :]
            m_new = jnp.maximum(m_old, jnp.where(chosen, jnp.max(s, axis=0, keepdims=True), NEG_INF))
            alpha = jnp.exp2(m_old - m_new)
            pt = jnp.exp2(s - jnp.where(chosen, m_new, -NEG_INF))
            m_sc[h:h + 1, :] = m_new
            l_sc[h:h + 1, :] = alpha * l_sc[h:h + 1, :] + jnp.sum(pt, axis=0, keepdims=True)
            acc_sc[head_rows(h), :] = alpha * acc_sc[head_rows(h), :] + jnp.dot(
                vt_ref[head_rows(h), pl.ds(r0, MOBA_BLOCK)], pt.astype(BF16), preferred_element_type=F32)

    scores(0, 0)

    def body(k, _):
        n = 2 * k
        scores(n + 1, 1)
        attend(n, 0)
        scores(jnp.minimum(n + 2, i), 0)
        attend(n + 1, 1)
        return 0

    lax.fori_loop(0, (i + 1) // 2, body, 0)
    out_t = acc_sc[...].reshape(N_HEADS, HEAD_DIM, tq) / l_sc[...][:, None, :]
    o_ref[...] = out_t.reshape(W_MIX, tq).T


def _moba_prompt(qa, ka16, vat16, kmean):
    s = qa.shape[0]
    assert s % Q_TILE == 0 and Q_TILE == MOBA_BLOCK
    nb = kmean.shape[0]
    kmean = jnp.pad(kmean, ((0, -nb % 8), (0, 0)))
    return pl.pallas_call(
        _moba_prompt_kernel,
        grid=(s // Q_TILE,),
        in_specs=[pl.BlockSpec((Q_TILE, W_MIX), lambda i: (i, 0)), _resident(ka16.shape), _resident(vat16.shape),
                  _resident(kmean.shape)],
        out_specs=pl.BlockSpec((Q_TILE, W_MIX), lambda i: (i, 0)),
        out_shape=jax.ShapeDtypeStruct((s, W_MIX), F32),
        scratch_shapes=[pltpu.VMEM((N_HEADS // 2, LANES, 2 * Q_TILE), BF16),
                        pltpu.VMEM((N_HEADS, kmean.shape[0], Q_TILE), F32),
                        pltpu.VMEM((N_HEADS, Q_TILE), F32), pltpu.VMEM((N_HEADS, Q_TILE), F32),
                        pltpu.VMEM((W_MIX, Q_TILE), F32),
                        pltpu.VMEM((2, N_HEADS // 2, MOBA_BLOCK, 2 * Q_TILE), F32)],
        compiler_params=_cparams(("parallel",)),
        name="moba_prompt",
    )(qa, ka16, vat16, kmean)


def _sb_tile(z, mask, carry, tri):
    t = jnp.log1p(jnp.exp(-jnp.abs(z)))
    log_beta = jnp.minimum(z, 0.0) - t
    log_keep = jnp.where(mask, -jnp.maximum(z, 0.0) - t, 0.0)
    after = _split_dot(log_keep, tri) + carry
    a = jnp.where(mask, jnp.exp(log_beta + after), 0.0)
    return a, carry + jnp.sum(log_keep, axis=1, keepdims=True)


def _sb_prompt_kernel(q_ref, k_ref, v_ref, tri_ref, o_ref, acc_sc):
    i = pl.program_id(0)
    tq = Q_TILE
    lane = lax.broadcasted_iota(jnp.int32, (tq, LANES), 1)
    row = lax.broadcasted_iota(jnp.int32, (tq, tq), 0)
    col = lax.broadcasted_iota(jnp.int32, (tq, tq), 1)
    tri = tri_ref[...]
    nt = (((1,), (1,)), ((), ()))
    qes = []
    for h in range(N_HEADS):
        q2 = q_ref[:, (h // 2) * LANES:(h // 2 + 1) * LANES]
        qes.append(jnp.where(lane // HEAD_DIM == h % 2, q2 * (1.0 / math.sqrt(HEAD_DIM)), 0.0).astype(BF16))
    acc_sc[...] = jnp.zeros_like(acc_sc)

    def cond(state):
        j, carries = state
        worst = carries[0]
        for c in carries[1:]:
            worst = jnp.maximum(worst, c)
        return (j >= 0) & (jnp.max(worst) > SB_UNDERFLOW)

    def body(state):
        j, carries = state
        r0 = pl.multiple_of(j * tq, tq)
        mask = (col + (j - i) * tq) < row
        pair = lambda ref, h: ref[pl.ds(r0, tq), (h // 2) * LANES:(h // 2 + 1) * LANES]
        zs = [lax.dot_general(qes[h], pair(k_ref, h), nt, preferred_element_type=F32) for h in range(N_HEADS)]
        log_betas, afters, new_carries = [], [], []
        for h in range(N_HEADS):
            t = jnp.log1p(jnp.exp(-jnp.abs(zs[h])))
            log_betas.append(jnp.minimum(zs[h], 0.0) - t)
            log_keep = jnp.where(mask, -jnp.maximum(zs[h], 0.0) - t, 0.0)
            afters.append(_split_dot(log_keep, tri) + carries[h])
            new_carries.append(carries[h] + jnp.sum(log_keep, axis=1, keepdims=True))
        pvs = []
        for h in range(N_HEADS):
            a = jnp.where(mask, jnp.exp(log_betas[h] + afters[h]), 0.0)
            pvs.append(jnp.dot(a.astype(BF16), pair(v_ref, h), preferred_element_type=F32))
        for p in range(N_HEADS // 2):
            acc_sc[:, p * LANES:(p + 1) * LANES] += jnp.where(lane < HEAD_DIM, pvs[2 * p], pvs[2 * p + 1])
        return j - 1, tuple(new_carries)

    lax.while_loop(cond, body, (i, tuple(jnp.zeros((tq, 1), F32) for _ in range(N_HEADS))))
    o_ref[...] = acc_sc[...]


def _tri(n):
    r = jnp.arange(n)
    return (r[:, None] > r[None, :]).astype(BF16)


def _sb_prompt(qb, kb16, vb16):
    s = qb.shape[0]
    assert s % Q_TILE == 0
    return pl.pallas_call(
        _sb_prompt_kernel,
        grid=(s // Q_TILE,),
        in_specs=[pl.BlockSpec((Q_TILE, W_MIX), lambda i: (i, 0)), _resident(kb16.shape), _resident(vb16.shape),
                  _resident((Q_TILE, Q_TILE))],
        out_specs=pl.BlockSpec((Q_TILE, W_MIX), lambda i: (i, 0)),
        out_shape=jax.ShapeDtypeStruct((s, W_MIX), F32),
        scratch_shapes=[pltpu.VMEM((Q_TILE, W_MIX), F32)],
        compiler_params=_cparams(("parallel",)),
        name="sb_prompt",
    )(qb, kb16, vb16, _tri(Q_TILE))


PAGES_PER_STEP = 32
ROWS = LANES


def _head_block_diag(q):
    t = q.shape[0]
    lanehead = lax.broadcasted_iota(jnp.int32, q.shape, 1) // HEAD_DIM
    parts = [jnp.where(lanehead == h, q, 0.0) for h in range(N_HEADS)]
    parts.append(jnp.zeros((ROWS - N_HEADS * t, q.shape[1]), F32))
    return jnp.concatenate(parts, axis=0)


def _head_diag_extract(full, t):
    lanehead = lax.broadcasted_iota(jnp.int32, (t, full.shape[1]), 1) // HEAD_DIM
    out = jnp.zeros((t, full.shape[1]), F32)
    for h in range(N_HEADS):
        out = out + jnp.where(lanehead == h, full[h * t:(h + 1) * t], 0.0)
    return out


def _pad_rows(x):
    return jnp.concatenate([x, jnp.zeros((ROWS - x.shape[0], x.shape[1]), x.dtype)], axis=0)


def _moba_sample_kernel(pt_ref, q_ref, knew_ref, vnew_ref, *rest, n_blocks, t_new):
    kp = rest[:PAGES_PER_STEP]
    vp = rest[PAGES_PER_STEP:2 * PAGES_PER_STEP]
    o_ref, qbd_sc, m_sc, l_sc, acc_sc, kmean_sc = rest[2 * PAGES_PER_STEP:]
    g = pl.program_id(1)
    lane = lax.broadcasted_iota(jnp.int32, (ROWS, LANES), 1)
    row = lax.broadcasted_iota(jnp.int32, (ROWS, LANES), 0)

    @pl.when(g == 0)
    def _():
        qbd_sc[...] = _head_block_diag(q_ref[0])
        m_sc[...] = jnp.zeros_like(m_sc)
        l_sc[...] = jnp.zeros_like(l_sc)
        kmean_sc[...] = jnp.zeros_like(kmean_sc)

    qbd = qbd_sc[...]
    qs = (qbd * (1.0 / math.sqrt(HEAD_DIM))).astype(BF16)
    pages_per_block = MOBA_BLOCK // PAGE_SIZE
    blocks_per_step = PAGES_PER_STEP // pages_per_block
    klane = lax.broadcasted_iota(jnp.int32, kmean_sc.shape, 1)
    m_new, l_new, km_new = m_sc[...], l_sc[...], kmean_sc[...]
    kts = [jnp.concatenate([kp[jj * pages_per_block + r][...] for r in range(pages_per_block)], axis=1)
           for jj in range(blocks_per_step)]
    scores = [jnp.dot(qs, kt.astype(BF16), preferred_element_type=F32) for kt in kts]
    for jj in range(blocks_per_step):
        n = g * blocks_per_step + jj
        kt, s = kts[jj], scores[jj]
        vt = jnp.concatenate([vp[jj * pages_per_block + r][...] for r in range(pages_per_block)], axis=1)
        m = jnp.max(s, axis=1, keepdims=True)
        p = jnp.exp(s - m)
        m_new = jnp.where(lane == n, m, m_new)
        l_new = jnp.where(lane == n, jnp.sum(p, axis=1, keepdims=True), l_new)
        acc_sc[n] = lax.dot_general(p.astype(BF16), vt.astype(BF16), (((1,), (1,)), ((), ())),
                                    preferred_element_type=F32)
        km_new = jnp.where(klane == n, jnp.mean(kt, axis=1, keepdims=True), km_new)
    m_sc[...] = m_new
    l_sc[...] = l_new
    kmean_sc[...] = km_new

    @pl.when(g == pl.num_programs(1) - 1)
    def _():
        gate = jnp.dot(qbd, kmean_sc[...], preferred_element_type=F32, precision=lax.Precision.HIGHEST)
        sel = _top_mask_lanes(jnp.where(lane < n_blocks, gate, NEG_INF), MOBA_TOPK)
        m_all = m_sc[...]
        s_own = lax.dot_general(qs, _pad_rows(knew_ref[0]).astype(BF16), (((1,), (1,)), ((), ())),
                                preferred_element_type=F32)
        s_own = jnp.where((lane <= row % t_new) & (lane < t_new), s_own, NEG_INF)
        m_own = jnp.max(s_own, axis=1, keepdims=True)
        p_own = jnp.exp(s_own - m_own)
        m_tot = jnp.maximum(jnp.max(jnp.where(sel, m_all, NEG_INF), axis=1, keepdims=True), m_own)
        wgt = jnp.where(sel, jnp.exp(m_all - m_tot), 0.0)
        w_own = jnp.exp(m_own - m_tot)
        l_tot = jnp.sum(wgt * l_sc[...], axis=1, keepdims=True) + w_own * jnp.sum(p_own, axis=1, keepdims=True)
        out = w_own * jnp.dot(p_own.astype(BF16), _pad_rows(vnew_ref[0]).astype(BF16), preferred_element_type=F32)
        for nb in range(n_blocks):
            out = out + wgt[:, nb:nb + 1] * acc_sc[nb]
        o_ref[0] = _head_diag_extract(out / l_tot, t_new)


def _page_specs(layer):
    def spec(r):
        return pl.BlockSpec((None, None, W_MIX, PAGE_SIZE),
                            lambda b, g, pt: (layer, pt[b, g * PAGES_PER_STEP + r], 0, 0))
    return [spec(r) for r in range(PAGES_PER_STEP)]


def _moba_sample(page_table, qa, knew, vnew, cache_k, cache_v, layer):
    nseq, t_new, _ = qa.shape
    n_pages = page_table.shape[1]
    past = n_pages * PAGE_SIZE
    assert past % MOBA_BLOCK == 0 and t_new <= MOBA_BLOCK and t_new % 8 == 0 and N_HEADS * t_new <= ROWS
    assert n_pages % PAGES_PER_STEP == 0 and past // MOBA_BLOCK <= LANES
    n_blocks = past // MOBA_BLOCK
    new_spec = pl.BlockSpec((1, t_new, W_MIX), lambda b, g, pt: (b, 0, 0))
    grid_spec = pltpu.PrefetchScalarGridSpec(
        num_scalar_prefetch=1,
        grid=(nseq, n_pages // PAGES_PER_STEP),
        in_specs=[new_spec] * 3 + _page_specs(layer) + _page_specs(layer),
        out_specs=new_spec,
        scratch_shapes=[pltpu.VMEM((ROWS, W_MIX), F32), pltpu.VMEM((ROWS, LANES), F32),
                        pltpu.VMEM((ROWS, LANES), F32), pltpu.VMEM((n_blocks, ROWS, W_MIX), F32),
                        pltpu.VMEM((W_MIX, LANES), F32)],
    )
    return pl.pallas_call(
        functools.partial(_moba_sample_kernel, n_blocks=n_blocks, t_new=t_new),
        grid_spec=grid_spec,
        out_shape=jax.ShapeDtypeStruct((nseq, t_new, W_MIX), F32),
        compiler_params=_cparams(("parallel", "arbitrary")),
        name="moba_sample",
    )(page_table, qa, knew, vnew, *([cache_k] * PAGES_PER_STEP), *([cache_v] * PAGES_PER_STEP))


SB_EAGER_PAGES = 2


def _sb_sample_kernel(pt_ref, q_ref, knew_ref, vnew_ref, k1_ref, k2_ref, v1_ref, v2_ref, kc_ref, vc_ref, tri_ref,
                      o_ref, kbuf, vbuf, sem, *, layer, n_pages, t_new):
    b = pl.program_id(0)
    lane = lax.broadcasted_iota(jnp.int32, (ROWS, LANES), 1)
    row = lax.broadcasted_iota(jnp.int32, (ROWS, LANES), 0)
    real_row = lax.broadcasted_iota(jnp.int32, (ROWS, 1), 0) < N_HEADS * t_new
    tri = tri_ref[...]
    qs = (_head_block_diag(q_ref[0]) * (1.0 / math.sqrt(HEAD_DIM))).astype(BF16)

    nt = (((1,), (1,)), ((), ()))

    def tile(kt, vt, carry, acc):
        z = jnp.dot(qs, kt.astype(BF16), preferred_element_type=F32)
        a, carry = _sb_tile(z, every, carry, tri)
        return carry, acc + lax.dot_general(a.astype(BF16), vt.astype(BF16), nt, preferred_element_type=F32)

    every = lane >= 0
    new_mask = (lane < row % t_new) & (lane < t_new)
    z = lax.dot_general(qs, _pad_rows(knew_ref[0]).astype(BF16), nt, preferred_element_type=F32)
    a, carry = _sb_tile(z, new_mask, jnp.zeros((ROWS, 1), F32), tri)
    acc = jnp.dot(a.astype(BF16), _pad_rows(vnew_ref[0]).astype(BF16), preferred_element_type=F32)
    carry, acc = tile(k1_ref[...], v1_ref[...], carry, acc)
    carry, acc = tile(k2_ref[...], v2_ref[...], carry, acc)

    def cond(state):
        p, carry, _ = state
        return (p >= 0) & (jnp.max(jnp.where(real_row, carry, NEG_INF)) > SB_UNDERFLOW)

    def body(state):
        p, carry, acc = state
        page = pt_ref[b, p]
        ck = pltpu.make_async_copy(kc_ref.at[layer, page], kbuf, sem.at[0])
        cv = pltpu.make_async_copy(vc_ref.at[layer, page], vbuf, sem.at[1])
        ck.start()
        cv.start()
        ck.wait()
        cv.wait()
        carry, acc = tile(kbuf[...], vbuf[...], carry, acc)
        return p - 1, carry, acc

    _, _, acc = lax.while_loop(cond, body, (jnp.int32(n_pages - SB_EAGER_PAGES - 1), carry, acc))
    o_ref[0] = _head_diag_extract(acc, t_new)


def _sb_sample(page_table, qb, knew, vnew, cache_k, cache_v, layer):
    nseq, t_new, _ = qb.shape
    n_pages = page_table.shape[1]
    assert n_pages >= SB_EAGER_PAGES and t_new % 8 == 0 and N_HEADS * t_new <= ROWS and t_new <= LANES
    new_spec = pl.BlockSpec((1, t_new, W_MIX), lambda b, pt: (b, 0, 0))
    page = lambda back: pl.BlockSpec((None, None, W_MIX, PAGE_SIZE),
                                     lambda b, pt: (layer, pt[b, n_pages - back], 0, 0))
    grid_spec = pltpu.PrefetchScalarGridSpec(
        num_scalar_prefetch=1,
        grid=(nseq,),
        in_specs=[new_spec] * 3 + [page(1), page(2), page(1), page(2),
                                   pl.BlockSpec(memory_space=pl.ANY), pl.BlockSpec(memory_space=pl.ANY),
                                   pl.BlockSpec((PAGE_SIZE, PAGE_SIZE), lambda b, pt: (0, 0))],
        out_specs=new_spec,
        scratch_shapes=[pltpu.VMEM((W_MIX, PAGE_SIZE), F32), pltpu.VMEM((W_MIX, PAGE_SIZE), F32),
                        pltpu.SemaphoreType.DMA((2,))],
    )
    return pl.pallas_call(
        functools.partial(_sb_sample_kernel, layer=layer, n_pages=n_pages, t_new=t_new),
        grid_spec=grid_spec,
        out_shape=jax.ShapeDtypeStruct((nseq, t_new, W_MIX), F32),
        compiler_params=_cparams(("parallel",)),
        name="sb_sample",
    )(page_table, qb, knew, vnew, cache_k, cache_k, cache_v, cache_v, cache_k, cache_v, _tri(PAGE_SIZE))


PEER_HALF = 128


def _postmix_kernel(x_ref, oa_ref, ob_ref, sga_ref, sgb_ref, wua_ref, wub_ref, wo_ref, gffn_ref, wpqt_ref, sk_ref,
                    h_ref, hnt_ref, s1t_ref, s2t_ref):
    ya = jnp.dot(oa_ref[...].astype(BF16), wua_ref[...], preferred_element_type=F32)
    yb = jnp.dot(ob_ref[...].astype(BF16), wub_ref[...], preferred_element_type=F32)
    merged = sga_ref[...] * ya + sgb_ref[...] * yb
    h = x_ref[...] + jnp.dot(merged.astype(BF16), wo_ref[...], preferred_element_type=F32)
    h_ref[...] = h
    ms = jnp.mean(h * h, axis=-1, keepdims=True)
    hn = (h * lax.rsqrt(ms + RMS_EPS)) * gffn_ref[...]
    hnt = hn.T.astype(BF16)
    hnt_ref[...] = hnt
    pqt = jnp.dot(wpqt_ref[...], hnt, preferred_element_type=F32)
    for hh in range(PEER_HEADS):
        r0 = hh * 2 * PEER_HALF
        s1t_ref[hh] = jnp.dot(sk_ref[hh, 0], pqt[r0:r0 + PEER_HALF].astype(BF16), preferred_element_type=F32)
        s2t_ref[hh] = jnp.dot(sk_ref[hh, 1], pqt[r0 + PEER_HALF:r0 + 2 * PEER_HALF].astype(BF16),
                              preferred_element_type=F32)


def _postmix(x, oa, ob, sga, sgb, wua16, wub16, wo16, gffn, wpqt16, sk16):
    t = x.shape[0]
    assert t % TOK_TILE == 0
    tile = lambda wd: pl.BlockSpec((TOK_TILE, wd), lambda i: (i, 0))
    score = pl.BlockSpec((PEER_HEADS, PEER_NKEYS, TOK_TILE), lambda i: (0, 0, i))
    return pl.pallas_call(
        _postmix_kernel,
        grid=(t // TOK_TILE,),
        in_specs=[tile(D_MODEL), tile(W_MIX), tile(W_MIX), tile(D_MODEL), tile(D_MODEL),
                  _resident(wua16.shape), _resident(wub16.shape), _resident(wo16.shape),
                  pl.BlockSpec((1, D_MODEL), lambda i: (0, 0)), _resident(wpqt16.shape), _resident(sk16.shape)],
        out_specs=[tile(D_MODEL), pl.BlockSpec((D_MODEL, TOK_TILE), lambda i: (0, i)), score, score],
        out_shape=[jax.ShapeDtypeStruct((t, D_MODEL), F32), jax.ShapeDtypeStruct((D_MODEL, t), BF16),
                   jax.ShapeDtypeStruct((PEER_HEADS, PEER_NKEYS, t), F32),
                   jax.ShapeDtypeStruct((PEER_HEADS, PEER_NKEYS, t), F32)],
        compiler_params=_cparams(("parallel",)),
        name="postmix",
    )(x, oa, ob, sga, sgb, wua16, wub16, wo16, gffn.reshape(1, -1), wpqt16, sk16)


ROUTE_LANES = 512
ROUTE_CHUNK = 128
NOT_RANKED = 127.0


def _top_rows_distinct(sa, sb, k):
    n, l = sa.shape
    kiota = lax.broadcasted_iota(jnp.int32, (k, l), 0)

    def step(r, s, vals):
        m = jnp.max(s, axis=0, keepdims=True)
        return jnp.where(s == m, -jnp.inf, s), jnp.where(kiota == r, m, vals)

    def body(r, carry):
        return step(r, *carry[:2]) + step(r, *carry[2:])

    left_a, vals_a, left_b, vals_b = lax.fori_loop(
        0, k, body, (sa, jnp.zeros((k, l), F32), sb, jnp.zeros((k, l), F32)))

    def ranks(s, left, vals):
        taken = left == -jnp.inf
        rank = jnp.zeros((n, l), jnp.int32)
        for i in range(k):
            rank = rank + jnp.where(vals[i:i + 1] > s, 1, 0)
        count = jnp.sum(jnp.where(taken, 1, 0), axis=0, keepdims=True)
        return jnp.where(taken, rank, k), count

    rank_a, count_a = ranks(sa, left_a, vals_a)
    rank_b, count_b = ranks(sb, left_b, vals_b)
    distinct = jnp.min(jnp.where((count_a == k) & (count_b == k), 1, 0)) == 1
    return rank_a, vals_a, rank_b, vals_b, distinct


def _top_rows_pair(sa, sb, k):
    n, l = sa.shape
    riota = lax.broadcasted_iota(jnp.int32, (n, l), 0)
    kiota = lax.broadcasted_iota(jnp.int32, (k, l), 0)

    def step(r, s, rank, vals):
        m = jnp.max(s, axis=0, keepdims=True)
        pick = riota == jnp.min(jnp.where(s == m, riota, n), axis=0, keepdims=True)
        return jnp.where(pick, -jnp.inf, s), jnp.where(pick, r, rank), jnp.where(kiota == r, m, vals)

    def body(r, carry):
        return step(r, *carry[:3]) + step(r, *carry[3:])

    init = (jnp.full((n, l), k, jnp.int32), jnp.zeros((k, l), F32))
    out = lax.fori_loop(0, k, body, (sa,) + init + (sb,) + init)
    return out[1], out[2], out[4], out[5]


def _pair_candidates(v1, v2, k):
    assert k == 16
    l = v1.shape[1]
    parts = [v1[0:1] + v2, v1[1:2] + v2[0:8]] + [v1[i:i + 1] + v2[0:8] for i in range(2, 8)] + [v1[8:16] + v2[0:1]]
    cand = jnp.concatenate(parts, axis=0)
    r = lax.broadcasted_iota(jnp.int32, cand.shape, 0)
    grp = r // 8
    last = cand.shape[0] // 8 - 1
    flat = jnp.where(grp <= 2, r, jnp.where(grp == last, (r - 8 * (last - 1)) * k, r + 8 * grp - k))
    return cand, flat


def _route_kernel(s1_ref, s2_ref, na_ref, wa_ref, rb_ref, e2_ref, rank_sc, vals_sc):
    k = PEER_TOPK

    def chunk(c, _):
        sl = pl.ds(pl.multiple_of(c * ROUTE_CHUNK, ROUTE_CHUNK), ROUTE_CHUNK)
        s1 = s1_ref[0, :, sl]
        s2 = s2_ref[0, :, sl]

        rank_sc[0], vals_sc[0], rank_sc[1], vals_sc[1], distinct = _top_rows_distinct(s1, s2, k)

        @pl.when(jnp.logical_not(distinct))
        def _():
            rank_sc[0], vals_sc[0], rank_sc[1], vals_sc[1] = _top_rows_pair(s1, s2, k)

        rank1, v1, rank2, v2 = rank_sc[0], vals_sc[0], rank_sc[1], vals_sc[1]
        cand, flat = _pair_candidates(v1, v2, k)
        l = cand.shape[1]
        kiota = lax.broadcasted_iota(jnp.int32, (k, l), 0)
        top = cand[0:1]

        def body(r, carry):
            cnd, cnt, z = carry
            m = jnp.max(cnd, axis=0, keepdims=True)
            idx = jnp.min(jnp.where(cnd == m, flat, k * k), axis=0, keepdims=True)
            cnt = jnp.where(kiota == idx // k, cnt + 1.0, cnt)
            return jnp.where(flat == idx, -jnp.inf, cnd), cnt, z + jnp.exp(m - top)

        _, cnt, z = lax.fori_loop(0, k, body, (cand, jnp.zeros((k, l), F32), jnp.zeros((1, l), F32)))
        na = jnp.zeros(s1.shape, F32)
        for i in range(k):
            na = jnp.where(rank1 == i, cnt[i:i + 1], na)
        na_ref[0, :, sl] = na
        wa_ref[0, :, sl] = jnp.where(rank1 < k, jnp.exp(s1 - v1[0:1]) / z, 0.0)
        rb_ref[0, :, sl] = jnp.where(rank2 < k, rank2.astype(F32), NOT_RANKED).astype(BF16)
        e2_ref[0, :, sl] = jnp.where(rank2 < k, jnp.exp(s2 - v2[0:1]), 0.0).astype(BF16)
        return 0

    lax.fori_loop(0, ROUTE_LANES // ROUTE_CHUNK, chunk, 0)


def _route(s1t, s2t):
    t = s1t.shape[2]
    assert t % ROUTE_LANES == 0
    blk = pl.BlockSpec((1, PEER_NKEYS, ROUTE_LANES), lambda i, h: (h, 0, i))
    f32 = jax.ShapeDtypeStruct(s1t.shape, F32)
    b16 = jax.ShapeDtypeStruct(s1t.shape, BF16)
    return pl.pallas_call(
        _route_kernel,
        grid=(t // ROUTE_LANES, PEER_HEADS),
        in_specs=[blk, blk],
        out_specs=[blk] * 4,
        out_shape=[f32, f32, b16, b16],
        scratch_shapes=[pltpu.VMEM((2, PEER_NKEYS, ROUTE_CHUNK), jnp.int32),
                        pltpu.VMEM((2, PEER_TOPK, ROUTE_CHUNK), F32)],
        compiler_params=_cparams(("parallel", "parallel")),
        name="peer_route",
    )(s1t, s2t)


PEER_TOK = 512
A_PER_STEP = 8
BF16_SUBLANES = 16


def _peer_kernel(hnt_ref, na_ref, wa_ref, rb_ref, e2_ref, u_ref, v_ref, o_ref, acc_sc):
    j = pl.program_id(1)

    @pl.when(j == 0)
    def _():
        acc_sc[...] = jnp.zeros_like(acc_sc)

    hnt = hnt_ref[...]
    tok = hnt.shape[1]

    def row16(ref, h, aa):
        slab = jnp.broadcast_to(ref[h, aa:aa + 1, :], (BF16_SUBLANES, tok)).astype(BF16)
        return jnp.concatenate([slab] * (PEER_NKEYS // BF16_SUBLANES), axis=0)

    parts = []
    for aa in range(A_PER_STEP):
        act = jnp.dot(u_ref[aa * PEER_NKEYS:(aa + 1) * PEER_NKEYS, :], hnt, preferred_element_type=F32)
        gel = 0.5 * act * (1.0 + lax.erf(act * (1.0 / math.sqrt(2.0))))
        gate = jnp.zeros(act.shape, BF16)
        for h in range(PEER_HEADS):
            gate = gate + jnp.where(rb_ref[h] < row16(na_ref, h, aa), e2_ref[h], 0.0) * row16(wa_ref, h, aa)
        parts.append(gate * gel.astype(BF16))
    pt = jnp.concatenate(parts, axis=0)
    acc_sc[...] += lax.dot_general(pt, v_ref[...], (((0,), (0,)), ((), ())), preferred_element_type=F32)

    @pl.when(j == pl.num_programs(1) - 1)
    def _():
        o_ref[...] = acc_sc[...]


def _peer(hnt, na, wa, rb, e2, u16, v16):
    t = hnt.shape[1]
    assert t % PEER_TOK == 0 and u16.shape[0] == PEER_NKEYS * PEER_NKEYS
    n_exp = A_PER_STEP * PEER_NKEYS
    route = pl.BlockSpec((PEER_HEADS, PEER_NKEYS, PEER_TOK), lambda i, j: (0, 0, i))
    a_rows = pl.BlockSpec((PEER_HEADS, A_PER_STEP, PEER_TOK), lambda i, j: (0, j, i))
    table = pl.BlockSpec((n_exp, D_MODEL), lambda i, j: (j, 0))
    return pl.pallas_call(
        _peer_kernel,
        grid=(t // PEER_TOK, PEER_NKEYS // A_PER_STEP),
        in_specs=[pl.BlockSpec((D_MODEL, PEER_TOK), lambda i, j: (0, i)), a_rows, a_rows, route, route, table, table],
        out_specs=pl.BlockSpec((PEER_TOK, D_MODEL), lambda i, j: (i, 0)),
        out_shape=jax.ShapeDtypeStruct((t, D_MODEL), F32),
        scratch_shapes=[pltpu.VMEM((PEER_TOK, D_MODEL), F32)],
        compiler_params=_cparams(("parallel", "arbitrary")),
        name="peer_experts",
    )(hnt, na, wa, rb, e2, u16, v16)


def _ple_kernel(h_ref, f_ref, p_ref, gple_ref, wg_ref, wp_ref, y_ref):
    h = h_ref[...] + f_ref[...]
    ms = jnp.mean(h * h, axis=-1, keepdims=True)
    hn = ((h * lax.rsqrt(ms + RMS_EPS)) * gple_ref[...]).astype(BF16)
    gate = jax.nn.sigmoid(jnp.dot(hn, wg_ref[...], preferred_element_type=F32))
    y_ref[...] = h + gate * jnp.dot(p_ref[...].astype(BF16), wp_ref[...], preferred_element_type=F32)


def _ple(h, ffn, p, gple, wg16, wp16):
    t = h.shape[0]
    assert t % TOK_TILE == 0
    tile = lambda wd: pl.BlockSpec((TOK_TILE, wd), lambda i: (i, 0))
    return pl.pallas_call(
        _ple_kernel,
        grid=(t // TOK_TILE,),
        in_specs=[tile(D_MODEL), tile(D_MODEL), tile(p.shape[1]), pl.BlockSpec((1, D_MODEL), lambda i: (0, 0)),
                  _resident(wg16.shape), _resident(wp16.shape)],
        out_specs=tile(D_MODEL),
        out_shape=jax.ShapeDtypeStruct((t, D_MODEL), F32),
        compiler_params=_cparams(("parallel",)),
        name="ple_gate",
    )(h, ffn, p, gple.reshape(1, -1), wg16, wp16)


def _post_mix(x, oa, ob, sga, sgb, p, wts):
    h, hnt, s1t, s2t = _postmix(x, oa, ob, sga, sgb, wts["wua"], wts["wub"], wts["wo"], wts["gffn"],
                                wts["wpqt"], wts["sk"])
    na, wa, rb, e2 = _route(s1t, s2t)
    ffn = _peer(hnt, na, wa, rb, e2, wts["u"], wts["v"])
    return _ple(h, ffn, p, wts["gple"], wts["wg"], wts["wp"])


def kernel(x_prompt, x_sample, cache_k_moba, cache_v_moba, cache_k_sb, cache_v_sb, page_table, p_prompt, p_sample,
           norm_mix, w_in, q_norm_moba, k_norm_moba, w_up_moba, w_up_sb, w_o, norm_ffn, w_peer_q, peer_sub_keys,
           peer_u, peer_v, norm_ple, w_ple_gate, w_ple):
    depth = w_in.shape[0]
    nb, s, _ = x_prompt.shape
    nseq, t_new, _ = x_sample.shape
    past = page_table.shape[1] * PAGE_SIZE
    pos_p = jnp.arange(s, dtype=jnp.int32)
    pos_s = jnp.tile(past + jnp.arange(t_new, dtype=jnp.int32), nseq)
    paged = lambda c: jnp.transpose(c, (0, 1, 3, 4, 2)).reshape(c.shape[0], c.shape[1], W_MIX, PAGE_SIZE)
    ckm, cvm, cks, cvs = paged(cache_k_moba), paged(cache_v_moba), paged(cache_k_sb), paged(cache_v_sb)
    heads = lambda a, lead: a.reshape(*lead, N_HEADS, HEAD_DIM)
    heads_t = lambda a: jnp.transpose(a.reshape(N_HEADS, HEAD_DIM, -1), (2, 0, 1))

    hp = [x_prompt[b] for b in range(nb)]
    hs = x_sample.reshape(nseq * t_new, D_MODEL)
    kv_p = [[] for _ in range(4)]
    kv_s = [[] for _ in range(4)]
    for l in range(depth):
        w_in16 = w_in[l].astype(BF16)
        wts = dict(wua=w_up_moba[l].astype(BF16), wub=w_up_sb[l].astype(BF16), wo=w_o[l].astype(BF16),
                   gffn=norm_ffn[l], wpqt=w_peer_q[l].T.astype(BF16), sk=peer_sub_keys[l].astype(BF16),
                   u=peer_u[l].astype(BF16), v=peer_v[l].astype(BF16), gple=norm_ple[l],
                   wg=w_ple_gate[l].astype(BF16), wp=w_ple[l].astype(BF16))
        layer_kv = [[] for _ in range(4)]
        for b in range(nb):
            (qa, ka, va, qb, kb, vb, sga, sgb, ka16, va16, kb16, vb16, kmean) = _inproj(
                hp[b], pos_p, norm_mix[l], w_in16, q_norm_moba[l], k_norm_moba[l], True)
            oa = _moba_prompt(qa, ka16, va16, kmean.reshape(-1, W_MIX))
            ob = _sb_prompt(qb, kb16, vb16)
            hp[b] = _post_mix(hp[b], oa, ob, sga, sgb, p_prompt[l, b], wts)
            for dst, a in zip(layer_kv, (ka, va, kb, vb)):
                dst.append(heads_t(a))
        for dst, src in zip(kv_p, layer_kv):
            dst.append(jnp.stack(src))

        (qa, ka, va, qb, kb, vb, sga, sgb, *_) = _inproj(hs, pos_s, norm_mix[l], w_in16, q_norm_moba[l],
                                                          k_norm_moba[l], False)
        seq = lambda a: a.reshape(nseq, t_new, W_MIX)
        oa = _moba_sample(page_table, seq(qa), seq(ka), seq(va), ckm, cvm, l)
        ob = _sb_sample(page_table, seq(qb), seq(kb), seq(vb), cks, cvs, l)
        hs = _post_mix(hs, oa.reshape(-1, W_MIX), ob.reshape(-1, W_MIX), sga, sgb,
                       p_sample[l].reshape(-1, p_sample.shape[-1]), wts)
        for dst, a in zip(kv_s, (ka, va, kb, vb)):
            dst.append(heads(a, (nseq, t_new)))

    return (jnp.stack(hp), hs.reshape(nseq, t_new, D_MODEL),
            *[jnp.stack(a) for a in kv_p], *[jnp.stack(a) for a in kv_s])
```
